```python
import jax, jax.numpy as jnp
from jax import lax
import numpy as np

D_MODEL = 2048
BATCH = 1
SEQ = 8192
DEPTH = 2

ATT_HEAD_DIM = 64
ATT_Q_HEADS = 16
ATT_KV_HEADS = 4
ATT_GROUP = ATT_Q_HEADS // ATT_KV_HEADS
WINDOW = 128
ATT_BLOCK = WINDOW
ATT_WIDTH = ATT_Q_HEADS * ATT_HEAD_DIM
KV_WIDTH = ATT_KV_HEADS * ATT_HEAD_DIM

POOL_WINDOWS = (2, 4, 8, 16)
POOL_GROUPS = len(POOL_WINDOWS)
POOL_WIDTH = D_MODEL // 2
POOL_GROUP_DIM = POOL_WIDTH // POOL_GROUPS

D_INNER = D_MODEL
SSM_HEAD_DIM = 64
SSM_HEADS = D_INNER // SSM_HEAD_DIM
SSM_GROUPS = 4
HEADS_PER_GROUP = SSM_HEADS // SSM_GROUPS
D_STATE = 128
CONV_K = 4
CHUNK = 128
CONV_CH = D_INNER + 2 * SSM_GROUPS * D_STATE

N_BRANCH = 3
D_FF = -(-8 * D_MODEL // (3 * 256)) * 256
EPS = 1e-6

IN_WIDTHS = (ATT_WIDTH, KV_WIDTH, KV_WIDTH, POOL_WIDTH, D_INNER, CONV_CH, SSM_HEADS, N_BRANCH * D_MODEL)
IN_COLS = sum(IN_WIDTHS)

kernel_name = 'hybrid_gated_swa_pool_ssd_block'


def split_points(widths):
    pts, acc = [], 0
    for w in widths[:-1]:
        acc += w
        pts.append(acc)
    return pts


def rmsnorm(x, w):
    xf = x.astype(jnp.float32)
    y = xf * lax.rsqrt(jnp.mean(xf * xf, axis=-1, keepdims=True) + EPS)
    return (y * w.astype(jnp.float32)).astype(x.dtype)


def sink_window_attention(q, k, v, sink):
    b, l = q.shape[0], q.shape[1]
    nb = l // ATT_BLOCK
    qb = q.reshape(b, nb, ATT_BLOCK, ATT_KV_HEADS, ATT_GROUP, ATT_HEAD_DIM)
    kb = k.reshape(b, nb, ATT_BLOCK, ATT_KV_HEADS, ATT_HEAD_DIM)
    vb = v.reshape(b, nb, ATT_BLOCK, ATT_KV_HEADS, ATT_HEAD_DIM)

    def with_prev(t):
        prev = jnp.pad(t[:, :-1], ((0, 0), (1, 0), (0, 0), (0, 0), (0, 0)))
        return jnp.concatenate([prev, t], axis=2)

    kk, vv = with_prev(kb), with_prev(vb)
    scores = jnp.einsum('bnqkgd,bnskd->bnkgqs', qb, kk).astype(jnp.float32) * (ATT_HEAD_DIM ** -0.5)
    blk = jnp.arange(nb)[:, None, None]
    qpos = blk * ATT_BLOCK + jnp.arange(ATT_BLOCK)[None, :, None]
    kpos = (blk - 1) * ATT_BLOCK + jnp.arange(2 * ATT_BLOCK)[None, None, :]
    diff = qpos - kpos
    mask = (diff >= 0) & (diff < WINDOW) & (kpos >= 0)
    scores = jnp.where(mask[None, :, None, None], scores, -jnp.inf)
    sink_l = sink.astype(jnp.float32).reshape(ATT_KV_HEADS, ATT_GROUP)[None, None, :, :, None, None]
    sink_l = jnp.broadcast_to(sink_l, scores.shape[:-1] + (1,))
    probs = jax.nn.softmax(jnp.concatenate([scores, sink_l], axis=-1), axis=-1)[..., :-1]
    out = jnp.einsum('bnkgqs,bnskd->bnqkgd', probs.astype(v.dtype), vv)
    return out.reshape(b, l, ATT_WIDTH)


def multiscale_pool(u, pool_w, pool_scale):
    b, l, _ = u.shape
    ug = u.reshape(b, l, POOL_GROUPS, POOL_GROUP_DIM).astype(jnp.float32)
    cs = jnp.cumsum(ug, axis=1)
    t = jnp.arange(l)
    means = []
    for gi, w in enumerate(POOL_WINDOWS):
        c = cs[:, :, gi]
        shifted = jnp.pad(c, ((0, 0), (w, 0), (0, 0)))[:, :l]
        cnt = jnp.minimum(t + 1, w).astype(jnp.float32)[None, :, None]
        means.append((c - shifted) / cnt)
    mixed = (jnp.stack(means, axis=2) - ug).astype(u.dtype)
    y = jnp.einsum('blgc,gcd->blgd', mixed, pool_w).reshape(b, l, POOL_WIDTH)
    return y * pool_scale


def ssd_mixer(z, xbc, dt_raw, conv_w, conv_b, dt_bias, a_log, d_skip, norm_w):
    b, l, _ = xbc.shape
    xbc = lax.conv_general_dilated(xbc, conv_w[:, None, :].astype(xbc.dtype), window_strides=(1,),
                                   padding=[(CONV_K - 1, 0)], dimension_numbers=('NWC', 'WIO', 'NWC'),
                                   feature_group_count=CONV_CH) + conv_b
    xbc = jax.nn.silu(xbc)
    nc = l // CHUNK
    xs = xbc[..., :D_INNER].astype(jnp.float32).reshape(b, nc, CHUNK, SSM_GROUPS, HEADS_PER_GROUP, SSM_HEAD_DIM)
    bm = xbc[..., D_INNER:D_INNER + SSM_GROUPS * D_STATE].astype(jnp.float32).reshape(b, nc, CHUNK, SSM_GROUPS, D_STATE)
    cm = xbc[..., D_INNER + SSM_GROUPS * D_STATE:].astype(jnp.float32).reshape(b, nc, CHUNK, SSM_GROUPS, D_STATE)
    dt = jax.nn.softplus(dt_raw.astype(jnp.float32) + dt_bias.astype(jnp.float32))
    dt = dt.reshape(b, nc, CHUNK, SSM_GROUPS, HEADS_PER_GROUP)
    a = -jnp.exp(a_log.astype(jnp.float32)).reshape(SSM_GROUPS, HEADS_PER_GROUP)
    a_cs = jnp.cumsum(dt * a, axis=2)
    xdt = xs * dt[..., None]
    seg = a_cs[:, :, :, None] - a_cs[:, :, None, :]
    causal = jnp.tril(jnp.ones((CHUNK, CHUNK), dtype=bool))
    decay = jnp.exp(jnp.where(causal[:, :, None, None], seg, -jnp.inf))
    cb = jnp.einsum('bclgn,bcsgn->bclsg', cm, bm)
    y_diag = jnp.einsum('bclsg,bclsge,bcsgep->bclgep', cb, decay, xdt)
    decay_to_end = jnp.exp(a_cs[:, :, -1:] - a_cs)
    states = jnp.einsum('bcsgn,bcsge,bcsgep->bcgepn', bm, decay_to_end, xdt)
    chunk_decay = jnp.exp(a_cs[:, :, -1])

    def step(h, inp):
        st, dec = inp
        return h * dec[..., None, None] + st, h

    h0 = jnp.zeros((b, SSM_GROUPS, HEADS_PER_GROUP, SSM_HEAD_DIM, D_STATE), jnp.float32)
    _, prev = lax.scan(step, h0, (jnp.moveaxis(states, 1, 0), jnp.moveaxis(chunk_decay, 1, 0)))
    prev = jnp.moveaxis(prev, 0, 1)
    y_off = jnp.einsum('bclgn,bcgepn,bclge->bclgep', cm, prev, jnp.exp(a_cs))
    y = y_diag + y_off + d_skip.astype(jnp.float32).reshape(SSM_GROUPS, HEADS_PER_GROUP)[:, :, None] * xs
    y = y.reshape(b, l, D_INNER)
    g = (y * jax.nn.silu(z.astype(jnp.float32))).reshape(b, l, SSM_GROUPS, D_INNER // SSM_GROUPS)
    g = g * lax.rsqrt(jnp.mean(g * g, axis=-1, keepdims=True) + EPS)
    return (g.reshape(b, l, D_INNER) * norm_w.astype(jnp.float32)).astype(z.dtype)


def setup_inputs(seed: int = 0) -> dict:
    key = jax.random.key(seed)
    ks = jax.random.split(key, 24)
    f32 = jnp.float32

    def nrm(k, shape, scale):
        return jax.random.normal(k, shape, f32) * scale

    dt0 = jnp.exp(jax.random.uniform(ks[6], (DEPTH, SSM_HEADS), f32, np.log(1e-3), np.log(1e-1)))
    dt_bias = dt0 + jnp.log(-jnp.expm1(-dt0))
    a_log = jnp.log(jax.random.uniform(ks[7], (DEPTH, SSM_HEADS), f32, 1.0, 16.0))
    return {
        'x': nrm(ks[0], (BATCH, SEQ, D_MODEL), 1.0),
        'ln1_w': 1.0 + nrm(ks[1], (DEPTH, D_MODEL), 0.05),
        'w_in': nrm(ks[2], (DEPTH, D_MODEL, IN_COLS), D_MODEL ** -0.5),
        'attn_sink': nrm(ks[3], (DEPTH, ATT_Q_HEADS), 0.5),
        'conv_w': nrm(ks[4], (DEPTH, CONV_K, CONV_CH), CONV_K ** -0.5),
        'conv_b': nrm(ks[5], (DEPTH, CONV_CH), 0.02),
        'dt_bias': dt_bias,
        'a_log': a_log,
        'd_skip': 1.0 + nrm(ks[8], (DEPTH, SSM_HEADS), 0.1),
        'ssm_norm_w': 1.0 + nrm(ks[9], (DEPTH, D_INNER), 0.05),
        'pool_w': nrm(ks[10], (DEPTH, POOL_GROUPS, POOL_GROUP_DIM, POOL_GROUP_DIM), POOL_GROUP_DIM ** -0.5),
        'pool_scale': 1.0 + nrm(ks[11], (DEPTH, POOL_WIDTH), 0.1),
        'w_attn_br': nrm(ks[12], (DEPTH, ATT_WIDTH, D_MODEL), ATT_WIDTH ** -0.5),
        'w_pool_br': nrm(ks[13], (DEPTH, POOL_WIDTH, D_MODEL), POOL_WIDTH ** -0.5),
        'w_ssm_br': nrm(ks[14], (DEPTH, D_INNER, D_MODEL), D_INNER ** -0.5),
        'w_out': nrm(ks[15], (DEPTH, D_MODEL, D_MODEL), D_MODEL ** -0.5),
        'ln2_w': 1.0 + nrm(ks[16], (DEPTH, D_MODEL), 0.05),
        'w_gate_up': nrm(ks[17], (DEPTH, D_MODEL, 2 * D_FF), D_MODEL ** -0.5),
        'w_down': nrm(ks[18], (DEPTH, D_FF, D_MODEL), D_FF ** -0.5),
        'final_w': 1.0 + nrm(ks[19], (D_MODEL,), 0.05),
    }


def reference(x, ln1_w, w_in, attn_sink, conv_w, conv_b, dt_bias, a_log, d_skip, ssm_norm_w,
              pool_w, pool_scale, w_attn_br, w_pool_br, w_ssm_br, w_out, ln2_w, w_gate_up,
              w_down, final_w):
    b, l, _ = x.shape
    pts = split_points(IN_WIDTHS)
    for i in range(DEPTH):
        h = rmsnorm(x, ln1_w[i])
        proj = h @ w_in[i]
        q, k, v, u, z, xbc, dt_raw, gate_logits = jnp.split(proj, pts, axis=-1)
        att = sink_window_attention(q.reshape(b, l, ATT_Q_HEADS, ATT_HEAD_DIM),
                                    k.reshape(b, l, ATT_KV_HEADS, ATT_HEAD_DIM),
                                    v.reshape(b, l, ATT_KV_HEADS, ATT_HEAD_DIM), attn_sink[i])
        pool = multiscale_pool(u, pool_w[i], pool_scale[i])
        ssm = ssd_mixer(z, xbc, dt_raw, conv_w[i], conv_b[i], dt_bias[i], a_log[i], d_skip[i], ssm_norm_w[i])
        gates = jax.nn.sigmoid(gate_logits.astype(jnp.float32)).astype(x.dtype).reshape(b, l, N_BRANCH, D_MODEL)
        merged = (gates[:, :, 0] * (att @ w_attn_br[i])
                  + gates[:, :, 1] * (pool @ w_pool_br[i])
                  + gates[:, :, 2] * (ssm @ w_ssm_br[i]))
        x = x + merged @ w_out[i]
        h2 = rmsnorm(x, ln2_w[i])
        gu = h2 @ w_gate_up[i]
        x = x + (jax.nn.silu(gu[..., :D_FF]) * gu[..., D_FF:]) @ w_down[i]
    return rmsnorm(x, final_w)
```

```python
import functools

import jax
import jax.numpy as jnp
from jax import lax
from jax.experimental import pallas as pl
from jax.experimental.pallas import tpu as pltpu

F32 = jnp.float32
BF16 = jnp.bfloat16

D_MODEL = 2048
ATT_HEAD_DIM = 64
ATT_Q_HEADS = 16
ATT_KV_HEADS = 4
ATT_GROUP = ATT_Q_HEADS // ATT_KV_HEADS
ATT_BLOCK = 128
ATT_WIDTH = ATT_Q_HEADS * ATT_HEAD_DIM
KV_WIDTH = ATT_KV_HEADS * ATT_HEAD_DIM
POOL_WINDOWS = (2, 4, 8, 16)
POOL_WIDTH = D_MODEL // 2
POOL_GROUP_DIM = POOL_WIDTH // len(POOL_WINDOWS)
POOL_HALO = 16
D_INNER = D_MODEL
SSM_HEAD_DIM = 64
SSM_HEADS = D_INNER // SSM_HEAD_DIM
SSM_GROUPS = 4
HEADS_PER_GROUP = SSM_HEADS // SSM_GROUPS
GROUP_CH = D_INNER // SSM_GROUPS
D_STATE = 128
CONV_K = 4
CONV_HALO = 8
CHUNK = 128
BC_WIDTH = 2 * SSM_GROUPS * D_STATE
N_BRANCH = 3
D_FF = 5632
EPS = 1e-6
DT_PAD = 128
MASK_VALUE = -1e30

COL_Q = 0
COL_U = 1024
COL_Z = 2048
COL_XS = 4096
COL_BC = 6144
COL_KV = 7168
COL_GATES = 7680
PROJ_COLS = COL_GATES + N_BRANCH * D_MODEL

VMEM_LIMIT_BYTES = 56 * 1024 * 1024


def _params(sem):
    return pltpu.CompilerParams(dimension_semantics=sem, vmem_limit_bytes=VMEM_LIMIT_BYTES)


def _silu(v):
    return v * jax.nn.sigmoid(v)


def _softplus(v):
    return jnp.maximum(v, 0.0) + jnp.log1p(jnp.exp(-jnp.abs(v)))


def _rmsnorm_rows(x, w):
    ms = jnp.mean(x * x, axis=-1, keepdims=True)
    return x * lax.rsqrt(ms + EPS) * w


def _inproj_kernel(x_ref, lnw_ref, w_ref, wdt_ref, o_ref, dt_ref, h_ref):
    @pl.when(pl.program_id(1) == 0)
    def _():
        h = _rmsnorm_rows(x_ref[...], lnw_ref[...]).astype(BF16)
        h_ref[...] = h
        dt_ref[...] = jnp.dot(h, wdt_ref[...], preferred_element_type=F32)

    o_ref[...] = jnp.dot(h_ref[...], w_ref[...], preferred_element_type=F32).astype(o_ref.dtype)


def _inproj(x, lnw, w, wdt, tm, tn):
    l = x.shape[0]
    return pl.pallas_call(
        _inproj_kernel,
        grid=(l // tm, PROJ_COLS // tn),
        in_specs=[
            pl.BlockSpec((tm, D_MODEL), lambda i, j: (i, 0)),
            pl.BlockSpec((1, D_MODEL), lambda i, j: (0, 0)),
            pl.BlockSpec((D_MODEL, tn), lambda i, j: (0, j)),
            pl.BlockSpec((D_MODEL, DT_PAD), lambda i, j: (0, 0)),
        ],
        out_specs=[
            pl.BlockSpec((tm, tn), lambda i, j: (i, j)),
            pl.BlockSpec((tm, DT_PAD), lambda i, j: (i, 0)),
        ],
        out_shape=[
            jax.ShapeDtypeStruct((l, PROJ_COLS), BF16),
            jax.ShapeDtypeStruct((l, DT_PAD), F32),
        ],
        scratch_shapes=[pltpu.VMEM((tm, D_MODEL), BF16)],
        compiler_params=_params(("arbitrary", "arbitrary")),
        name="inproj",
    )(x, lnw, w, wdt)


def _attn_kernel(sink_ref, q_ref, kv_ref, kvp_ref, o_ref):
    n = pl.program_id(0)
    rows = ATT_GROUP * ATT_BLOCK
    qi = lax.broadcasted_iota(jnp.int32, (rows, 2 * ATT_BLOCK), 0) % ATT_BLOCK
    kj = lax.broadcasted_iota(jnp.int32, (rows, 2 * ATT_BLOCK), 1)
    mask = (kj > qi) & (kj <= qi + ATT_BLOCK) & ((kj >= ATT_BLOCK) | (n > 0))
    row_head = lax.broadcasted_iota(jnp.int32, (rows, 1), 0) // ATT_BLOCK

    q = q_ref[...]
    kv_cur = kv_ref[...]
    kv_prev = kvp_ref[...]
    outs = []
    for kh in range(ATT_KV_HEADS):
        ksl = slice(kh * ATT_HEAD_DIM, (kh + 1) * ATT_HEAD_DIM)
        vsl = slice(KV_WIDTH + kh * ATT_HEAD_DIM, KV_WIDTH + (kh + 1) * ATT_HEAD_DIM)
        kk = jnp.concatenate([kv_prev[:, ksl], kv_cur[:, ksl]], axis=0)
        vv = jnp.concatenate([kv_prev[:, vsl], kv_cur[:, vsl]], axis=0)
        qs = jnp.concatenate(
            [q[:, (kh * ATT_GROUP + g) * ATT_HEAD_DIM:(kh * ATT_GROUP + g + 1) * ATT_HEAD_DIM]
             for g in range(ATT_GROUP)], axis=0)
        s = lax.dot_general(qs, kk, (((1,), (1,)), ((), ())), preferred_element_type=F32)
        s = jnp.where(mask, s * (ATT_HEAD_DIM ** -0.5), MASK_VALUE)
        sink = jnp.zeros((rows, 1), F32)
        for g in range(ATT_GROUP):
            sink = jnp.where(row_head == g, sink_ref[kh * ATT_GROUP + g], sink)
        m = jnp.maximum(jnp.max(s, axis=-1, keepdims=True), sink)
        p = jnp.exp(s - m)
        denom = jnp.sum(p, axis=-1, keepdims=True) + jnp.exp(sink - m)
        o = jnp.dot(p.astype(BF16), vv, preferred_element_type=F32) / denom
        for g in range(ATT_GROUP):
            outs.append(o[g * ATT_BLOCK:(g + 1) * ATT_BLOCK])
    o_ref[...] = jnp.concatenate(outs, axis=1).astype(o_ref.dtype)


def _attention(proj, sink):
    l = proj.shape[0]
    kv_col = COL_KV // (2 * KV_WIDTH)
    return pl.pallas_call(
        _attn_kernel,
        grid=(l // ATT_BLOCK,),
        in_specs=[
            pl.BlockSpec(memory_space=pltpu.SMEM),
            pl.BlockSpec((ATT_BLOCK, ATT_WIDTH), lambda n: (n, COL_Q // ATT_WIDTH)),
            pl.BlockSpec((ATT_BLOCK, 2 * KV_WIDTH), lambda n: (n, kv_col)),
            pl.BlockSpec((ATT_BLOCK, 2 * KV_WIDTH), lambda n: (jnp.maximum(n - 1, 0), kv_col)),
        ],
        out_specs=pl.BlockSpec((ATT_BLOCK, ATT_WIDTH), lambda n: (n, 0)),
        out_shape=jax.ShapeDtypeStruct((l, ATT_WIDTH), BF16),
        compiler_params=_params(("arbitrary",)),
        name="attention",
    )(sink, proj, proj, proj)


def _pool_kernel(u_ref, pw_ref, ps_ref, o_ref, ue_ref, *, tile):
    i = pl.program_id(0)

    @pl.when(i == 0)
    def _():
        ue_ref[0:POOL_HALO, :] = jnp.zeros((POOL_HALO, POOL_WIDTH), F32)

    ue_ref[POOL_HALO:POOL_HALO + tile, :] = u_ref[...].astype(F32)
    t = i * tile + lax.broadcasted_iota(jnp.int32, (tile, 1), 0)
    for g, w in enumerate(POOL_WINDOWS):
        cols = slice(g * POOL_GROUP_DIM, (g + 1) * POOL_GROUP_DIM)
        cur = ue_ref[POOL_HALO:POOL_HALO + tile, cols]
        acc = cur
        for j in range(1, w):
            acc = acc + ue_ref[POOL_HALO - j:POOL_HALO - j + tile, cols]
        cnt = jnp.minimum(t + 1, w).astype(F32)
        mixed = (acc / cnt - cur).astype(BF16)
        y = jnp.dot(mixed, pw_ref[g], preferred_element_type=F32) * ps_ref[:, cols]
        o_ref[:, cols] = y.astype(o_ref.dtype)
    ue_ref[0:POOL_HALO, :] = ue_ref[tile:tile + POOL_HALO, :]


def _pool(proj, pool_w, pool_scale, tile):
    l = proj.shape[0]
    return pl.pallas_call(
        functools.partial(_pool_kernel, tile=tile),
        grid=(l // tile,),
        in_specs=[
            pl.BlockSpec((tile, POOL_WIDTH), lambda i: (i, COL_U // POOL_WIDTH)),
            pl.BlockSpec((len(POOL_WINDOWS), POOL_GROUP_DIM, POOL_GROUP_DIM), lambda i: (0, 0, 0)),
            pl.BlockSpec((1, POOL_WIDTH), lambda i: (0, 0)),
        ],
        out_specs=pl.BlockSpec((tile, POOL_WIDTH), lambda i: (i, 0)),
        out_shape=jax.ShapeDtypeStruct((l, POOL_WIDTH), BF16),
        scratch_shapes=[pltpu.VMEM((tile + POOL_HALO, POOL_WIDTH), F32)],
        compiler_params=_params(("arbitrary",)),
        name="pool",
    )(proj, pool_w, pool_scale)


def _expand_heads(v, g):
    lane = lax.broadcasted_iota(jnp.int32, (CHUNK, 2 * SSM_HEAD_DIM), 1)
    tiles = []
    for k in range(HEADS_PER_GROUP // 2):
        h0 = g * HEADS_PER_GROUP + 2 * k
        c0 = jnp.broadcast_to(v[:, h0:h0 + 1], (CHUNK, 2 * SSM_HEAD_DIM))
        c1 = jnp.broadcast_to(v[:, h0 + 1:h0 + 2], (CHUNK, 2 * SSM_HEAD_DIM))
        tiles.append(jnp.where(lane < SSM_HEAD_DIM, c0, c1))
    return jnp.concatenate(tiles, axis=1)


def _causal_conv_silu(ext_ref, cur, w_ref, b_ref):
    ext_ref[CONV_HALO:CONV_HALO + CHUNK, :] = cur
    acc = b_ref[...] + w_ref[CONV_K - 1:CONV_K, :] * cur
    for k in range(CONV_K - 1):
        off = CONV_HALO - (CONV_K - 1) + k
        acc = acc + w_ref[k:k + 1, :] * ext_ref[off:off + CHUNK, :]
    ext_ref[0:CONV_HALO, :] = ext_ref[CHUNK:CHUNK + CONV_HALO, :]
    return _silu(acc)


def _ssd_kernel(xs_ref, bc_ref, z_ref, dt_ref, cwx_ref, cbx_ref, cwbc_ref, cbbc_ref, dtb_ref, alog_ref,
                dskip_ref, nw_ref, o_ref, st_ref, xe_ref, bce_ref, y_ref):
    c = pl.program_id(0)

    @pl.when(c == 0)
    def _():
        st_ref[...] = jnp.zeros_like(st_ref)
        xe_ref[0:CONV_HALO, :] = jnp.zeros((CONV_HALO, D_INNER), F32)
        bce_ref[0:CONV_HALO, :] = jnp.zeros((CONV_HALO, BC_WIDTH), F32)

    xs = _causal_conv_silu(xe_ref, xs_ref[...].astype(F32), cwx_ref, cbx_ref)
    bcs = _causal_conv_silu(bce_ref, bc_ref[...].astype(F32), cwbc_ref, cbbc_ref).astype(BF16)

    dt = _softplus(dt_ref[...] + dtb_ref[...])
    da = dt * (-jnp.exp(alog_ref[...]))
    ri = lax.broadcasted_iota(jnp.int32, (CHUNK, CHUNK), 0)
    ci = lax.broadcasted_iota(jnp.int32, (CHUNK, CHUNK), 1)
    causal = ri >= ci
    a_cs = jnp.dot(causal.astype(F32), da, precision=lax.Precision.HIGHEST, preferred_element_type=F32)
    a_cs_t = a_cs.T
    ecs = jnp.exp(a_cs)
    dte = jnp.exp(a_cs[CHUNK - 1:CHUNK, :] - a_cs)

    for g in range(SSM_GROUPS):
        gsl = slice(g * GROUP_CH, (g + 1) * GROUP_CH)
        b_g = bcs[:, g * D_STATE:(g + 1) * D_STATE]
        c_g = bcs[:, (SSM_GROUPS + g) * D_STATE:(SSM_GROUPS + g + 1) * D_STATE]
        cb = lax.dot_general(c_g, b_g, (((1,), (1,)), ((), ())), preferred_element_type=F32)
        x_g = xs[:, gsl]
        xdt = x_g * _expand_heads(dt, g)
        xdt_b = xdt.astype(BF16)
        xdte_b = (xdt * _expand_heads(dte, g)).astype(BF16)
        for e in range(HEADS_PER_GROUP):
            h = g * HEADS_PER_GROUP + e
            seg = a_cs[:, h:h + 1] - a_cs_t[h:h + 1, :]
            decay = jnp.exp(jnp.where(causal, seg, -jnp.inf))
            m = (cb * decay).astype(BF16)
            y_ref[:, h * SSM_HEAD_DIM:(h + 1) * SSM_HEAD_DIM] = jnp.dot(
                m, xdt_b[:, e * SSM_HEAD_DIM:(e + 1) * SSM_HEAD_DIM], preferred_element_type=F32)
        ecs_x = _expand_heads(ecs, g)
        s_prev = st_ref[g]
        y_off = jnp.dot(c_g, s_prev.astype(BF16), preferred_element_type=F32) * ecs_x
        s_new = lax.dot_general(b_g, xdte_b, (((0,), (0,)), ((), ())), preferred_element_type=F32)
        st_ref[g] = s_prev * ecs_x[CHUNK - 1:CHUNK, :] + s_new
        y = y_ref[:, gsl] + y_off + dskip_ref[:, gsl] * x_g
        gated = y * _silu(z_ref[:, gsl].astype(F32))
        o_ref[:, gsl] = _rmsnorm_rows(gated, nw_ref[:, gsl]).astype(o_ref.dtype)


def _ssd(proj, dt_raw, cwx, cbx, cwbc, cbbc, dtb, alog, dskip, nw):
    l = proj.shape[0]
    full = lambda shape: pl.BlockSpec(shape, lambda c: (0, 0))
    return pl.pallas_call(
        _ssd_kernel,
        grid=(l // CHUNK,),
        in_specs=[
            pl.BlockSpec((CHUNK, D_INNER), lambda c: (c, COL_XS // D_INNER)),
            pl.BlockSpec((CHUNK, BC_WIDTH), lambda c: (c, COL_BC // BC_WIDTH)),
            pl.BlockSpec((CHUNK, D_INNER), lambda c: (c, COL_Z // D_INNER)),
            pl.BlockSpec((CHUNK, DT_PAD), lambda c: (c, 0)),
            full((CONV_K, D_INNER)), full((1, D_INNER)),
            full((CONV_K, BC_WIDTH)), full((1, BC_WIDTH)),
            full((1, DT_PAD)), full((1, DT_PAD)),
            full((1, D_INNER)), full((1, D_INNER)),
        ],
        out_specs=pl.BlockSpec((CHUNK, D_INNER), lambda c: (c, 0)),
        out_shape=jax.ShapeDtypeStruct((l, D_INNER), BF16),
        scratch_shapes=[
            pltpu.VMEM((SSM_GROUPS, D_STATE, GROUP_CH), F32),
            pltpu.VMEM((CHUNK + CONV_HALO, D_INNER), F32),
            pltpu.VMEM((CHUNK + CONV_HALO, BC_WIDTH), F32),
            pltpu.VMEM((CHUNK, D_INNER), F32),
        ],
        compiler_params=_params(("arbitrary",)),
        name="ssd",
    )(proj, proj, proj, dt_raw, cwx, cbx, cwbc, cbbc, dtb, alog, dskip, nw)


def _merge_kernel(att_ref, pool_ref, ssm_ref, g0_ref, g1_ref, g2_ref, wa_ref, wp_ref, ws_ref, o_ref):
    def branch(a_ref, w_ref, g_ref):
        return jax.nn.sigmoid(g_ref[...].astype(F32)) * jnp.dot(a_ref[...], w_ref[...], preferred_element_type=F32)

    merged = branch(att_ref, wa_ref, g0_ref) + branch(pool_ref, wp_ref, g1_ref) + branch(ssm_ref, ws_ref, g2_ref)
    o_ref[...] = merged.astype(o_ref.dtype)


def _merge(att, pool, ssm, proj, wa, wp, ws, tm, tn):
    l = att.shape[0]
    gate_spec = lambda b: pl.BlockSpec((tm, tn), lambda i, j: (i, (COL_GATES + b * D_MODEL) // tn + j))
    return pl.pallas_call(
        _merge_kernel,
        grid=(l // tm, D_MODEL // tn),
        in_specs=[
            pl.BlockSpec((tm, ATT_WIDTH), lambda i, j: (i, 0)),
            pl.BlockSpec((tm, POOL_WIDTH), lambda i, j: (i, 0)),
            pl.BlockSpec((tm, D_INNER), lambda i, j: (i, 0)),
            gate_spec(0), gate_spec(1), gate_spec(2),
            pl.BlockSpec((ATT_WIDTH, tn), lambda i, j: (0, j)),
            pl.BlockSpec((POOL_WIDTH, tn), lambda i, j: (0, j)),
            pl.BlockSpec((D_INNER, tn), lambda i, j: (0, j)),
        ],
        out_specs=pl.BlockSpec((tm, tn), lambda i, j: (i, j)),
        out_shape=jax.ShapeDtypeStruct((l, D_MODEL), BF16),
        compiler_params=_params(("arbitrary", "arbitrary")),
        name="merge",
    )(att, pool, ssm, proj, proj, proj, wa, wp, ws)


def _outproj_kernel(m_ref, w_ref, x_ref, o_ref):
    o_ref[...] = x_ref[...] + jnp.dot(m_ref[...], w_ref[...], preferred_element_type=F32)


def _outproj(merged, w_out, x, tm, tn):
    l = x.shape[0]
    return pl.pallas_call(
        _outproj_kernel,
        grid=(l // tm, D_MODEL // tn),
        in_specs=[
            pl.BlockSpec((tm, D_MODEL), lambda i, j: (i, 0)),
            pl.BlockSpec((D_MODEL, tn), lambda i, j: (0, j)),
            pl.BlockSpec((tm, tn), lambda i, j: (i, j)),
        ],
        out_specs=pl.BlockSpec((tm, tn), lambda i, j: (i, j)),
        out_shape=jax.ShapeDtypeStruct((l, D_MODEL), F32),
        compiler_params=_params(("arbitrary", "arbitrary")),
        name="outproj",
    )(merged, w_out, x)


def _ffn_kernel(x_ref, lnw_ref, wg_ref, wu_ref, wd_ref, fw_ref, o_ref, h_ref, acc_ref, *, final_norm):
    j = pl.program_id(1)

    @pl.when(j == 0)
    def _():
        x = x_ref[...]
        h_ref[...] = _rmsnorm_rows(x, lnw_ref[...]).astype(BF16)
        acc_ref[...] = x

    h = h_ref[...]
    gate = jnp.dot(h, wg_ref[...], preferred_element_type=F32)
    up = jnp.dot(h, wu_ref[...], preferred_element_type=F32)
    act = (_silu(gate) * up).astype(BF16)
    acc_ref[...] += jnp.dot(act, wd_ref[...], preferred_element_type=F32)

    @pl.when(j == pl.num_programs(1) - 1)
    def _():
        y = acc_ref[...]
        if final_norm:
            y = _rmsnorm_rows(y, fw_ref[...])
        o_ref[...] = y


def _ffn(x, lnw, w_gate_up, w_down, final_w, tm, tf, final_norm):
    l = x.shape[0]
    nf = D_FF // tf
    return pl.pallas_call(
        functools.partial(_ffn_kernel, final_norm=final_norm),
        grid=(l // tm, nf),
        in_specs=[
            pl.BlockSpec((tm, D_MODEL), lambda i, j: (i, 0)),
            pl.BlockSpec((1, D_MODEL), lambda i, j: (0, 0)),
            pl.BlockSpec((D_MODEL, tf), lambda i, j: (0, j)),
            pl.BlockSpec((D_MODEL, tf), lambda i, j: (0, nf + j)),
            pl.BlockSpec((tf, D_MODEL), lambda i, j: (j, 0)),
            pl.BlockSpec((1, D_MODEL), lambda i, j: (0, 0)),
        ],
        out_specs=pl.BlockSpec((tm, D_MODEL), lambda i, j: (i, 0)),
        out_shape=jax.ShapeDtypeStruct((l, D_MODEL), F32),
        scratch_shapes=[pltpu.VMEM((tm, D_MODEL), BF16), pltpu.VMEM((tm, D_MODEL), F32)],
        compiler_params=_params(("arbitrary", "arbitrary")),
        name="ffn",
    )(x, lnw, w_gate_up, w_gate_up, w_down, final_w)


def _split_w_in(w):
    q, k, v, u, z, xs, bc, dtw, gates = jnp.split(
        w, [1024, 1280, 1536, 2560, 4608, 6656, 7680, 7712], axis=1)
    main = jnp.concatenate([q, u, z, xs, bc, k, v, gates], axis=1).astype(BF16)
    wdt = jnp.pad(dtw, ((0, 0), (0, DT_PAD - SSM_HEADS))).astype(BF16)
    return main, wdt


def _pad_heads(v):
    return jnp.pad(v, (0, DT_PAD - SSM_HEADS)).reshape(1, DT_PAD)


def kernel(x, ln1_w, w_in, attn_sink, conv_w, conv_b, dt_bias, a_log, d_skip, ssm_norm_w, pool_w, pool_scale,
           w_attn_br, w_pool_br, w_ssm_br, w_out, ln2_w, w_gate_up, w_down, final_w):
    b, l, _ = x.shape
    depth = w_in.shape[0]
    assert l % CHUNK == 0 and CHUNK == ATT_BLOCK
    tm = min(512, l)
    row = lambda v: v.reshape(1, -1)
    outs = []
    for bi in range(b):
        xb = x[bi]
        for i in range(depth):
            w_main, w_dt = _split_w_in(w_in[i])
            proj, dt_raw = _inproj(xb, row(ln1_w[i]), w_main, w_dt, tm, 1536)
            att = _attention(proj, attn_sink[i])
            pool = _pool(proj, pool_w[i].astype(BF16), row(pool_scale[i]), tm)
            ssm = _ssd(proj, dt_raw,
                       conv_w[i][:, :D_INNER], row(conv_b[i][:D_INNER]),
                       conv_w[i][:, D_INNER:], row(conv_b[i][D_INNER:]),
                       _pad_heads(dt_bias[i]), _pad_heads(a_log[i]),
                       row(jnp.repeat(d_skip[i], SSM_HEAD_DIM)), row(ssm_norm_w[i]))
            merged = _merge(att, pool, ssm, proj, w_attn_br[i].astype(BF16), w_pool_br[i].astype(BF16),
                            w_ssm_br[i].astype(BF16), tm, 512)
            xb = _outproj(merged, w_out[i].astype(BF16), xb, tm, 1024)
            xb = _ffn(xb, row(ln2_w[i]), w_gate_up[i].astype(BF16), w_down[i].astype(BF16), row(final_w),
                      tm, 512, final_norm=(i == depth - 1))
        outs.append(xb)
    return jnp.stack(outs, axis=0)
```

```python
import functools

import jax
import jax.numpy as jnp
from jax import lax
from jax.experimental import pallas as pl
from jax.experimental.pallas import tpu as pltpu

F32 = jnp.float32
BF16 = jnp.bfloat16

D_MODEL = 2048
ATT_HEAD_DIM = 64
ATT_Q_HEADS = 16
ATT_KV_HEADS = 4
ATT_GROUP = ATT_Q_HEADS // ATT_KV_HEADS
ATT_BLOCK = 128
ATT_WIDTH = ATT_Q_HEADS * ATT_HEAD_DIM
KV_WIDTH = ATT_KV_HEADS * ATT_HEAD_DIM
POOL_WINDOWS = (2, 4, 8, 16)
POOL_WIDTH = D_MODEL // 2
POOL_GROUP_DIM = POOL_WIDTH // len(POOL_WINDOWS)
POOL_HALO = 16
D_INNER = D_MODEL
SSM_HEAD_DIM = 64
SSM_HEADS = D_INNER // SSM_HEAD_DIM
SSM_GROUPS = 4
HEADS_PER_GROUP = SSM_HEADS // SSM_GROUPS
GROUP_CH = D_INNER // SSM_GROUPS
D_STATE = 128
CONV_K = 4
CONV_HALO = 8
CHUNK = 128
BC_WIDTH = 2 * SSM_GROUPS * D_STATE
N_BRANCH = 3
D_FF = 5632
EPS = 1e-6
DT_PAD = 128
MASK_VALUE = -1e30

COL_Q = 0
COL_U = 1024
COL_Z = 2048
COL_XS = 4096
COL_BC = 6144
COL_GATES = 7168
COL_KV = COL_GATES + N_BRANCH * D_MODEL
PROJ_COLS = COL_KV + 2 * KV_WIDTH

VMEM_LIMIT_BYTES = 56 * 1024 * 1024


def _params(sem):
    return pltpu.CompilerParams(dimension_semantics=sem, vmem_limit_bytes=VMEM_LIMIT_BYTES)


def _silu(v):
    return v * jax.nn.sigmoid(v)


def _softplus(v):
    return jnp.maximum(v, 0.0) + jnp.log1p(jnp.exp(-jnp.abs(v)))


def _rmsnorm_rows(x, w):
    ms = jnp.mean(x * x, axis=-1, keepdims=True)
    return x * lax.rsqrt(ms + EPS) * w


def _inproj_kernel(x_ref, lnw_ref, w_ref, wdt_ref, o_ref, dt_ref, h_ref):
    @pl.when(pl.program_id(1) == 0)
    def _():
        h = _rmsnorm_rows(x_ref[...], lnw_ref[...]).astype(BF16)
        h_ref[...] = h
        dt_ref[...] = jnp.dot(h, wdt_ref[...], preferred_element_type=F32)

    o_ref[...] = jnp.dot(h_ref[...], w_ref[...], preferred_element_type=F32).astype(o_ref.dtype)


def _inproj(x, lnw, w, wdt, tm, tn):
    l = x.shape[0]
    return pl.pallas_call(
        _inproj_kernel,
        grid=(l // tm, PROJ_COLS // tn),
        in_specs=[
            pl.BlockSpec((tm, D_MODEL), lambda i, j: (i, 0)),
            pl.BlockSpec((1, D_MODEL), lambda i, j: (0, 0)),
            pl.BlockSpec((D_MODEL, tn), lambda i, j: (0, j)),
            pl.BlockSpec((D_MODEL, DT_PAD), lambda i, j: (0, 0)),
        ],
        out_specs=[
            pl.BlockSpec((tm, tn), lambda i, j: (i, j)),
            pl.BlockSpec((tm, DT_PAD), lambda i, j: (i, 0)),
        ],
        out_shape=[
            jax.ShapeDtypeStruct((l, PROJ_COLS), BF16),
            jax.ShapeDtypeStruct((l, DT_PAD), F32),
        ],
        scratch_shapes=[pltpu.VMEM((tm, D_MODEL), BF16)],
        compiler_params=_params(("arbitrary", "arbitrary")),
        name="inproj",
    )(x, lnw, w, wdt)


def _attn_kernel(sink_ref, q_ref, kv_ref, kvp_ref, o_ref):
    n = pl.program_id(0)
    rows = ATT_GROUP * ATT_BLOCK
    qi = lax.broadcasted_iota(jnp.int32, (rows, 2 * ATT_BLOCK), 0) % ATT_BLOCK
    kj = lax.broadcasted_iota(jnp.int32, (rows, 2 * ATT_BLOCK), 1)
    mask = (kj > qi) & (kj <= qi + ATT_BLOCK) & ((kj >= ATT_BLOCK) | (n > 0))
    row_head = lax.broadcasted_iota(jnp.int32, (rows, 1), 0) // ATT_BLOCK

    q = q_ref[...]
    kv_cur = kv_ref[...]
    kv_prev = kvp_ref[...]
    outs = []
    for kh in range(ATT_KV_HEADS):
        ksl = slice(kh * ATT_HEAD_DIM, (kh + 1) * ATT_HEAD_DIM)
        vsl = slice(KV_WIDTH + kh * ATT_HEAD_DIM, KV_WIDTH + (kh + 1) * ATT_HEAD_DIM)
        kk = jnp.concatenate([kv_prev[:, ksl], kv_cur[:, ksl]], axis=0)
        vv = jnp.concatenate([kv_prev[:, vsl], kv_cur[:, vsl]], axis=0)
        qs = jnp.concatenate(
            [q[:, (kh * ATT_GROUP + g) * ATT_HEAD_DIM:(kh * ATT_GROUP + g + 1) * ATT_HEAD_DIM]
             for g in range(ATT_GROUP)], axis=0)
        s = lax.dot_general(qs, kk, (((1,), (1,)), ((), ())), preferred_element_type=F32)
        s = jnp.where(mask, s * (ATT_HEAD_DIM ** -0.5), MASK_VALUE)
        sink = jnp.zeros((rows, 1), F32)
        for g in range(ATT_GROUP):
            sink = jnp.where(row_head == g, sink_ref[kh * ATT_GROUP + g], sink)
        m = jnp.maximum(jnp.max(s, axis=-1, keepdims=True), sink)
        p = jnp.exp(s - m)
        denom = jnp.sum(p, axis=-1, keepdims=True) + jnp.exp(sink - m)
        o = jnp.dot(p.astype(BF16), vv, preferred_element_type=F32) / denom
        for g in range(ATT_GROUP):
            outs.append(o[g * ATT_BLOCK:(g + 1) * ATT_BLOCK])
    o_ref[...] = jnp.concatenate(outs, axis=1).astype(o_ref.dtype)


def _attention(proj, sink):
    l = proj.shape[0]
    kv_col = COL_KV // (2 * KV_WIDTH)
    return pl.pallas_call(
        _attn_kernel,
        grid=(l // ATT_BLOCK,),
        in_specs=[
            pl.BlockSpec(memory_space=pltpu.SMEM),
            pl.BlockSpec((ATT_BLOCK, ATT_WIDTH), lambda n: (n, COL_Q // ATT_WIDTH)),
            pl.BlockSpec((ATT_BLOCK, 2 * KV_WIDTH), lambda n: (n, kv_col)),
            pl.BlockSpec((ATT_BLOCK, 2 * KV_WIDTH), lambda n: (jnp.maximum(n - 1, 0), kv_col)),
        ],
        out_specs=pl.BlockSpec((ATT_BLOCK, ATT_WIDTH), lambda n: (n, 0)),
        out_shape=jax.ShapeDtypeStruct((l, ATT_WIDTH), BF16),
        compiler_params=_params(("arbitrary",)),
        name="attention",
    )(sink, proj, proj, proj)


def _pool_kernel(u_ref, pw_ref, ps_ref, o_ref, ue_ref, *, tile):
    i = pl.program_id(0)

    @pl.when(i == 0)
    def _():
        ue_ref[0:POOL_HALO, :] = jnp.zeros((POOL_HALO, POOL_WIDTH), F32)

    ue_ref[POOL_HALO:POOL_HALO + tile, :] = u_ref[...].astype(F32)
    t = i * tile + lax.broadcasted_iota(jnp.int32, (tile, 1), 0)
    for g, w in enumerate(POOL_WINDOWS):
        cols = slice(g * POOL_GROUP_DIM, (g + 1) * POOL_GROUP_DIM)
        cur = ue_ref[POOL_HALO:POOL_HALO + tile, cols]
        acc = cur
        for j in range(1, w):
            acc = acc + ue_ref[POOL_HALO - j:POOL_HALO - j + tile, cols]
        cnt = jnp.minimum(t + 1, w).astype(F32)
        mixed = (acc / cnt - cur).astype(BF16)
        y = jnp.dot(mixed, pw_ref[g], preferred_element_type=F32) * ps_ref[:, cols]
        o_ref[:, cols] = y.astype(o_ref.dtype)
    ue_ref[0:POOL_HALO, :] = ue_ref[tile:tile + POOL_HALO, :]


def _pool(proj, pool_w, pool_scale, tile):
    l = proj.shape[0]
    return pl.pallas_call(
        functools.partial(_pool_kernel, tile=tile),
        grid=(l // tile,),
        in_specs=[
            pl.BlockSpec((tile, POOL_WIDTH), lambda i: (i, COL_U // POOL_WIDTH)),
            pl.BlockSpec((len(POOL_WINDOWS), POOL_GROUP_DIM, POOL_GROUP_DIM), lambda i: (0, 0, 0)),
            pl.BlockSpec((1, POOL_WIDTH), lambda i: (0, 0)),
        ],
        out_specs=pl.BlockSpec((tile, POOL_WIDTH), lambda i: (i, 0)),
        out_shape=jax.ShapeDtypeStruct((l, POOL_WIDTH), BF16),
        scratch_shapes=[pltpu.VMEM((tile + POOL_HALO, POOL_WIDTH), F32)],
        compiler_params=_params(("arbitrary",)),
        name="pool",
    )(proj, pool_w, pool_scale)


def _split3(v):
    p1 = v.astype(BF16)
    r1 = v - p1.astype(F32)
    p2 = r1.astype(BF16)
    p3 = (r1 - p2.astype(F32)).astype(BF16)
    return jnp.concatenate([p1, p2, p3], axis=1)


def _head_expander():
    k = lax.broadcasted_iota(jnp.int32, (3 * DT_PAD, D_INNER), 0) % DT_PAD
    ch = lax.broadcasted_iota(jnp.int32, (3 * DT_PAD, D_INNER), 1) // SSM_HEAD_DIM
    return (k == ch).astype(BF16)


def _causal_conv_silu(ext_ref, cur, w_ref, b_ref):
    ext_ref[CONV_HALO:CONV_HALO + CHUNK, :] = cur
    acc = b_ref[...] + w_ref[CONV_K - 1:CONV_K, :] * cur
    for k in range(CONV_K - 1):
        off = CONV_HALO - (CONV_K - 1) + k
        acc = acc + w_ref[k:k + 1, :] * ext_ref[off:off + CHUNK, :]
    ext_ref[0:CONV_HALO, :] = ext_ref[CHUNK:CHUNK + CONV_HALO, :]
    return _silu(acc)


def _ssd_kernel(xs_ref, bc_ref, z_ref, dt_ref, cwx_ref, cbx_ref, cwbc_ref, cbbc_ref, dtb_ref, alog_ref,
                dskip_ref, nw_ref, ex_ref, o_ref, st_ref, xe_ref, bce_ref):
    c = pl.program_id(0)

    @pl.when(c == 0)
    def _():
        st_ref[...] = jnp.zeros_like(st_ref)
        xe_ref[0:CONV_HALO, :] = jnp.zeros((CONV_HALO, D_INNER), F32)
        bce_ref[0:CONV_HALO, :] = jnp.zeros((CONV_HALO, BC_WIDTH), F32)

    xs = _causal_conv_silu(xe_ref, xs_ref[...].astype(F32), cwx_ref, cbx_ref)
    bcs = _causal_conv_silu(bce_ref, bc_ref[...].astype(F32), cwbc_ref, cbbc_ref).astype(BF16)
    xs_b = xs.astype(BF16)

    dt = _softplus(dt_ref[...] + dtb_ref[...])
    da = dt * (-jnp.exp(alog_ref[...]))
    ri = lax.broadcasted_iota(jnp.int32, (CHUNK, CHUNK), 0)
    ci = lax.broadcasted_iota(jnp.int32, (CHUNK, CHUNK), 1)
    causal = ri >= ci
    a_cs = jnp.dot(causal.astype(F32), da, precision=lax.Precision.HIGHEST, preferred_element_type=F32)
    a_cs_t = a_cs.T
    dt_t = dt.T
    ecs = jnp.exp(a_cs)
    a_last = a_cs[CHUNK - 1:CHUNK, :]
    per_head = jnp.concatenate([dt * jnp.exp(a_last - a_cs), jnp.broadcast_to(jnp.exp(a_last), (8, DT_PAD))], axis=0)
    per_ch = jnp.dot(_split3(per_head), ex_ref[...], preferred_element_type=F32)
    xdte_b = (xs * per_ch[0:CHUNK]).astype(BF16)
    chunk_decay = per_ch[CHUNK:CHUNK + 1]

    pr = lax.broadcasted_iota(jnp.int32, (4 * CHUNK, 2 * SSM_HEAD_DIM), 0) < 2 * CHUNK
    pc = lax.broadcasted_iota(jnp.int32, (4 * CHUNK, 2 * SSM_HEAD_DIM), 1) < SSM_HEAD_DIM
    pair_mask = pr == pc

    for g in range(SSM_GROUPS):
        gsl = slice(g * GROUP_CH, (g + 1) * GROUP_CH)
        b_g = bcs[:, g * D_STATE:(g + 1) * D_STATE]
        c_g = bcs[:, (SSM_GROUPS + g) * D_STATE:(SSM_GROUPS + g + 1) * D_STATE]
        cb = lax.dot_general(c_g, b_g, (((1,), (1,)), ((), ())), preferred_element_type=F32)
        cb = jnp.where(causal, cb, 0.0)
        c_f = c_g.astype(F32)
        s_prev = st_ref[g]
        s_prev_b = s_prev.astype(BF16)
        tiles = []
        for p in range(HEADS_PER_GROUP // 2):
            parts = []
            for h in (g * HEADS_PER_GROUP + 2 * p, g * HEADS_PER_GROUP + 2 * p + 1):
                a_col = jnp.broadcast_to(a_cs[:, h:h + 1], (CHUNK, CHUNK))
                decay = jnp.exp(jnp.minimum(a_col - a_cs_t[h:h + 1, :], 0.0))
                parts.append(cb * decay * dt_t[h:h + 1, :])
                parts.append(c_f * jnp.broadcast_to(ecs[:, h:h + 1], (CHUNK, CHUNK)))
            lhs = jnp.concatenate(parts, axis=1).astype(BF16)
            psl = slice(p * 2 * SSM_HEAD_DIM, (p + 1) * 2 * SSM_HEAD_DIM)
            x_p = xs_b[:, g * GROUP_CH + p * 2 * SSM_HEAD_DIM:g * GROUP_CH + (p + 1) * 2 * SSM_HEAD_DIM]
            s_p = s_prev_b[:, psl]
            rhs = jnp.concatenate([x_p, s_p, x_p, s_p], axis=0)
            rhs = jnp.where(pair_mask, rhs, jnp.zeros_like(rhs))
            tiles.append(jnp.dot(lhs, rhs, preferred_element_type=F32))
        s_new = lax.dot_general(b_g, xdte_b[:, gsl], (((0,), (0,)), ((), ())), preferred_element_type=F32)
        st_ref[g] = s_prev * chunk_decay[:, gsl] + s_new
        y = jnp.concatenate(tiles, axis=1) + dskip_ref[:, gsl] * xs[:, gsl]
        gated = y * _silu(z_ref[:, gsl].astype(F32))
        o_ref[:, gsl] = _rmsnorm_rows(gated, nw_ref[:, gsl]).astype(o_ref.dtype)


def _ssd(proj, dt_raw, cwx, cbx, cwbc, cbbc, dtb, alog, dskip, nw):
    l = proj.shape[0]
    full = lambda shape: pl.BlockSpec(shape, lambda c: (0, 0))
    return pl.pallas_call(
        _ssd_kernel,
        grid=(l // CHUNK,),
        in_specs=[
            pl.BlockSpec((CHUNK, D_INNER), lambda c: (c, COL_XS // D_INNER)),
            pl.BlockSpec((CHUNK, BC_WIDTH), lambda c: (c, COL_BC // BC_WIDTH)),
            pl.BlockSpec((CHUNK, D_INNER), lambda c: (c, COL_Z // D_INNER)),
            pl.BlockSpec((CHUNK, DT_PAD), lambda c: (c, 0)),
            full((CONV_K, D_INNER)), full((1, D_INNER)),
            full((CONV_K, BC_WIDTH)), full((1, BC_WIDTH)),
            full((1, DT_PAD)), full((1, DT_PAD)),
            full((1, D_INNER)), full((1, D_INNER)),
            full((3 * DT_PAD, D_INNER)),
        ],
        out_specs=pl.BlockSpec((CHUNK, D_INNER), lambda c: (c, 0)),
        out_shape=jax.ShapeDtypeStruct((l, D_INNER), BF16),
        scratch_shapes=[
            pltpu.VMEM((SSM_GROUPS, D_STATE, GROUP_CH), F32),
            pltpu.VMEM((CHUNK + CONV_HALO, D_INNER), F32),
            pltpu.VMEM((CHUNK + CONV_HALO, BC_WIDTH), F32),
        ],
        compiler_params=_params(("arbitrary",)),
        name="ssd",
    )(proj, proj, proj, dt_raw, cwx, cbx, cwbc, cbbc, dtb, alog, dskip, nw, _head_expander())


def _merge_kernel(att_ref, pool_ref, ssm_ref, g0_ref, g1_ref, g2_ref, wa_ref, wp_ref, ws_ref, o_ref):
    def branch(a_ref, w_ref, g_ref):
        return jax.nn.sigmoid(g_ref[...].astype(F32)) * jnp.dot(a_ref[...], w_ref[...], preferred_element_type=F32)

    merged = branch(att_ref, wa_ref, g0_ref) + branch(pool_ref, wp_ref, g1_ref) + branch(ssm_ref, ws_ref, g2_ref)
    o_ref[...] = merged.astype(o_ref.dtype)


def _merge(att, pool, ssm, proj, wa, wp, ws, tm, tn):
    l = att.shape[0]
    assert COL_GATES % tn == 0 and D_MODEL % tn == 0 and l % tm == 0
    gate_spec = lambda b: pl.BlockSpec((tm, tn), lambda i, j: (i, (COL_GATES + b * D_MODEL) // tn + j))
    return pl.pallas_call(
        _merge_kernel,
        grid=(l // tm, D_MODEL // tn),
        in_specs=[
            pl.BlockSpec((tm, ATT_WIDTH), lambda i, j: (i, 0)),
            pl.BlockSpec((tm, POOL_WIDTH), lambda i, j: (i, 0)),
            pl.BlockSpec((tm, D_INNER), lambda i, j: (i, 0)),
            gate_spec(0), gate_spec(1), gate_spec(2),
            pl.BlockSpec((ATT_WIDTH, tn), lambda i, j: (0, j)),
            pl.BlockSpec((POOL_WIDTH, tn), lambda i, j: (0, j)),
            pl.BlockSpec((D_INNER, tn), lambda i, j: (0, j)),
        ],
        out_specs=pl.BlockSpec((tm, tn), lambda i, j: (i, j)),
        out_shape=jax.ShapeDtypeStruct((l, D_MODEL), BF16),
        compiler_params=_params(("arbitrary", "arbitrary")),
        name="merge",
    )(att, pool, ssm, proj, proj, proj, wa, wp, ws)


def _outproj_kernel(m_ref, w_ref, x_ref, o_ref):
    o_ref[...] = x_ref[...] + jnp.dot(m_ref[...], w_ref[...], preferred_element_type=F32)


def _outproj(merged, w_out, x, tm, tn):
    l = x.shape[0]
    return pl.pallas_call(
        _outproj_kernel,
        grid=(l // tm, D_MODEL // tn),
        in_specs=[
            pl.BlockSpec((tm, D_MODEL), lambda i, j: (i, 0)),
            pl.BlockSpec((D_MODEL, tn), lambda i, j: (0, j)),
            pl.BlockSpec((tm, tn), lambda i, j: (i, j)),
        ],
        out_specs=pl.BlockSpec((tm, tn), lambda i, j: (i, j)),
        out_shape=jax.ShapeDtypeStruct((l, D_MODEL), F32),
        compiler_params=_params(("arbitrary", "arbitrary")),
        name="outproj",
    )(merged, w_out, x)


def _ffn_kernel(x_ref, lnw_ref, wg_ref, wu_ref, wd_ref, fw_ref, o_ref, h_ref, acc_ref, *, final_norm):
    j = pl.program_id(1)

    @pl.when(j == 0)
    def _():
        x = x_ref[...]
        h_ref[...] = _rmsnorm_rows(x, lnw_ref[...]).astype(BF16)
        acc_ref[...] = x

    h = h_ref[...]
    gate = jnp.dot(h, wg_ref[...], preferred_element_type=F32)
    up = jnp.dot(h, wu_ref[...], preferred_element_type=F32)
    act = (_silu(gate) * up).astype(BF16)
    acc_ref[...] += jnp.dot(act, wd_ref[...], preferred_element_type=F32)

    @pl.when(j == pl.num_programs(1) - 1)
    def _():
        y = acc_ref[...]
        if final_norm:
            y = _rmsnorm_rows(y, fw_ref[...])
        o_ref[...] = y


def _ffn(x, lnw, w_gate_up, w_down, final_w, tm, tf, final_norm):
    l = x.shape[0]
    nf = D_FF // tf
    return pl.pallas_call(
        functools.partial(_ffn_kernel, final_norm=final_norm),
        grid=(l // tm, nf),
        in_specs=[
            pl.BlockSpec((tm, D_MODEL), lambda i, j: (i, 0)),
            pl.BlockSpec((1, D_MODEL), lambda i, j: (0, 0)),
            pl.BlockSpec((D_MODEL, tf), lambda i, j: (0, j)),
            pl.BlockSpec((D_MODEL, tf), lambda i, j: (0, nf + j)),
            pl.BlockSpec((tf, D_MODEL), lambda i, j: (j, 0)),
            pl.BlockSpec((1, D_MODEL), lambda i, j: (0, 0)),
        ],
        out_specs=pl.BlockSpec((tm, D_MODEL), lambda i, j: (i, 0)),
        out_shape=jax.ShapeDtypeStruct((l, D_MODEL), F32),
        scratch_shapes=[pltpu.VMEM((tm, D_MODEL), BF16), pltpu.VMEM((tm, D_MODEL), F32)],
        compiler_params=_params(("arbitrary", "arbitrary")),
        name="ffn",
    )(x, lnw, w_gate_up, w_gate_up, w_down, final_w)


def _split_w_in(w):
    q, k, v, u, z, xs, bc, dtw, gates = jnp.split(
        w, [1024, 1280, 1536, 2560, 4608, 6656, 7680, 7712], axis=1)
    main = jnp.concatenate([q, u, z, xs, bc, gates, k, v], axis=1).astype(BF16)
    wdt = jnp.pad(dtw, ((0, 0), (0, DT_PAD - SSM_HEADS))).astype(BF16)
    return main, wdt


def _pad_heads(v):
    return jnp.pad(v, (0, DT_PAD - SSM_HEADS)).reshape(1, DT_PAD)


def kernel(x, ln1_w, w_in, attn_sink, conv_w, conv_b, dt_bias, a_log, d_skip, ssm_norm_w, pool_w, pool_scale,
           w_attn_br, w_pool_br, w_ssm_br, w_out, ln2_w, w_gate_up, w_down, final_w):
    b, l, _ = x.shape
    depth = w_in.shape[0]
    assert l % CHUNK == 0 and CHUNK == ATT_BLOCK
    tm = min(512, l)
    tm_big = min(1024, l)
    row = lambda v: v.reshape(1, -1)
    outs = []
    for bi in range(b):
        xb = x[bi]
        for i in range(depth):
            w_main, w_dt = _split_w_in(w_in[i])
            proj, dt_raw = _inproj(xb, row(ln1_w[i]), w_main, w_dt, tm_big, 1536)
            att = _attention(proj, attn_sink[i])
            pool = _pool(proj, pool_w[i].astype(BF16), row(pool_scale[i]), tm)
            ssm = _ssd(proj, dt_raw,
                       conv_w[i][:, :D_INNER], row(conv_b[i][:D_INNER]),
                       conv_w[i][:, D_INNER:], row(conv_b[i][D_INNER:]),
                       _pad_heads(dt_bias[i]), _pad_heads(a_log[i]),
                       row(jnp.repeat(d_skip[i], SSM_HEAD_DIM)), row(ssm_norm_w[i]))
            merged = _merge(att, pool, ssm, proj, w_attn_br[i].astype(BF16), w_pool_br[i].astype(BF16),
                            w_ssm_br[i].astype(BF16), tm_big, 1024)
            xb = _outproj(merged, w_out[i].astype(BF16), xb, tm_big, 2048)
            xb = _ffn(xb, row(ln2_w[i]), w_gate_up[i].astype(BF16), w_down[i].astype(BF16), row(final_w),
                      tm, 512, final_norm=(i == depth - 1))
        outs.append(xb)
    return jnp.stack(outs, axis=0)
```

```python
import functools

import jax
import jax.numpy as jnp
from jax import lax
from jax.experimental import pallas as pl
from jax.experimental.pallas import tpu as pltpu

F32 = jnp.float32
BF16 = jnp.bfloat16

D_MODEL = 2048
ATT_HEAD_DIM = 64
ATT_Q_HEADS = 16
ATT_KV_HEADS = 4
ATT_GROUP = ATT_Q_HEADS // ATT_KV_HEADS
ATT_BLOCK = 128
ATT_WIDTH = ATT_Q_HEADS * ATT_HEAD_DIM
KV_WIDTH = ATT_KV_HEADS * ATT_HEAD_DIM
POOL_WINDOWS = (2, 4, 8, 16)
POOL_WIDTH = D_MODEL // 2
POOL_GROUP_DIM = POOL_WIDTH // len(POOL_WINDOWS)
POOL_HALO = 16
D_INNER = D_MODEL
SSM_HEAD_DIM = 64
SSM_HEADS = D_INNER // SSM_HEAD_DIM
SSM_GROUPS = 4
HEADS_PER_GROUP = SSM_HEADS // SSM_GROUPS
GROUP_CH = D_INNER // SSM_GROUPS
D_STATE = 128
CONV_K = 4
CONV_HALO = 8
CHUNK = 128
BC_WIDTH = 2 * SSM_GROUPS * D_STATE
N_BRANCH = 3
D_FF = 5632
EPS = 1e-6
DT_PAD = 128
MASK_VALUE = -1e30
LOG2_E = 1.4426950408889634

W_MAIN_COLS = 7680
W_COL_DT = W_MAIN_COLS
W_COL_GATES = W_COL_DT + SSM_HEADS
GATE_COLS = N_BRANCH * D_MODEL
COL_GATES = 0
COL_Q = GATE_COLS
COL_KV = COL_Q + ATT_WIDTH
COL_U = COL_KV + 2 * KV_WIDTH
COL_Z = COL_U + POOL_WIDTH
COL_XS = COL_Z + D_INNER
COL_BC = COL_XS + D_INNER
PROJ_COLS = COL_BC + BC_WIDTH
HALF = 512

VMEM_LIMIT_BYTES = 56 * 1024 * 1024


def _params(sem):
    return pltpu.CompilerParams(dimension_semantics=sem, vmem_limit_bytes=VMEM_LIMIT_BYTES)


def _silu(v):
    h = 0.5 * v
    return h + h * jnp.tanh(h)


def _softplus(v):
    return jnp.maximum(v, 0.0) + jnp.log1p(jnp.exp(-jnp.abs(v)))


def _rmsnorm_rows(x, w):
    ms = jnp.mean(x * x, axis=-1, keepdims=True)
    return x * lax.rsqrt(ms + EPS) * w


def _inproj_kernel(x_ref, lnw_ref, w_ref, wdt_ref, o_ref, dt_ref, h_ref):
    @pl.when(pl.program_id(1) == 0)
    def _():
        h = _rmsnorm_rows(x_ref[...], lnw_ref[...]).astype(BF16)
        h_ref[...] = h
        dt_ref[...] = jnp.dot(h, wdt_ref[...], preferred_element_type=F32)

    o_ref[...] = jnp.dot(h_ref[...], w_ref[...], preferred_element_type=F32).astype(o_ref.dtype)


def _inproj(x, lnw, w, wdt, tm, tn):
    l = x.shape[0]
    assert PROJ_COLS % tn == 0 and l % tm == 0
    return pl.pallas_call(
        _inproj_kernel,
        grid=(l // tm, PROJ_COLS // tn),
        in_specs=[
            pl.BlockSpec((tm, D_MODEL), lambda i, j: (i, 0)),
            pl.BlockSpec((1, D_MODEL), lambda i, j: (0, 0)),
            pl.BlockSpec((D_MODEL, tn), lambda i, j: (0, j)),
            pl.BlockSpec((D_MODEL, DT_PAD), lambda i, j: (0, 0)),
        ],
        out_specs=[
            pl.BlockSpec((tm, tn), lambda i, j: (i, j)),
            pl.BlockSpec((tm, DT_PAD), lambda i, j: (i, 0)),
        ],
        out_shape=[
            jax.ShapeDtypeStruct((l, PROJ_COLS), BF16),
            jax.ShapeDtypeStruct((l, DT_PAD), F32),
        ],
        scratch_shapes=[pltpu.VMEM((tm, D_MODEL), BF16)],
        compiler_params=_params(("arbitrary", "arbitrary")),
        name="inproj",
    )(x, lnw, w, wdt)


def _final_norm_kernel(x_ref, w_ref, o_ref):
    o_ref[...] = _rmsnorm_rows(x_ref[...], w_ref[...])


def _final_norm(x, w, tm):
    l = x.shape[0]
    return pl.pallas_call(
        _final_norm_kernel,
        grid=(l // tm,),
        in_specs=[pl.BlockSpec((tm, D_MODEL), lambda i: (i, 0)), pl.BlockSpec((1, D_MODEL), lambda i: (0, 0))],
        out_specs=pl.BlockSpec((tm, D_MODEL), lambda i: (i, 0)),
        out_shape=jax.ShapeDtypeStruct((l, D_MODEL), F32),
        compiler_params=_params(("arbitrary",)),
        name="final_norm",
    )(x, w)


def _cast_at_first_row_tile(w_ref, wb_ref):
    @pl.when(pl.program_id(1) == 0)
    def _():
        wb_ref[...] = w_ref[...].astype(BF16)


def _attn_kernel(sink_ref, q_ref, kv_ref, kvp_ref, o_ref):
    n = pl.program_id(0)
    rows = ATT_GROUP * ATT_BLOCK
    qi = lax.broadcasted_iota(jnp.int32, (rows, ATT_BLOCK), 0) % ATT_BLOCK
    kj = lax.broadcasted_iota(jnp.int32, (rows, ATT_BLOCK), 1)
    from_prev = kj > qi
    prev_bias = jnp.where(n > 0, 0.0, MASK_VALUE)
    row_head = lax.broadcasted_iota(jnp.int32, (rows, 1), 0) // ATT_BLOCK

    q = q_ref[...] * (ATT_HEAD_DIM ** -0.5)
    kv_cur = kv_ref[...]
    kv_prev = kvp_ref[...]
    outs = []
    for kh in range(ATT_KV_HEADS):
        ksl = slice(kh * ATT_HEAD_DIM, (kh + 1) * ATT_HEAD_DIM)
        vsl = slice(KV_WIDTH + kh * ATT_HEAD_DIM, KV_WIDTH + (kh + 1) * ATT_HEAD_DIM)
        kk = jnp.concatenate([kv_prev[:, ksl], kv_cur[:, ksl]], axis=0)
        vv = jnp.concatenate([kv_prev[:, vsl], kv_cur[:, vsl]], axis=0)
        qs = jnp.concatenate(
            [q[:, (kh * ATT_GROUP + g) * ATT_HEAD_DIM:(kh * ATT_GROUP + g + 1) * ATT_HEAD_DIM]
             for g in range(ATT_GROUP)], axis=0)
        s = lax.dot_general(qs, kk, (((1,), (1,)), ((), ())), preferred_element_type=F32)
        s = jnp.where(from_prev, s[:, :ATT_BLOCK] + prev_bias, s[:, ATT_BLOCK:])
        sink = jnp.zeros((rows, 1), F32)
        for g in range(ATT_GROUP):
            sink = jnp.where(row_head == g, sink_ref[kh * ATT_GROUP + g], sink)
        m = jnp.maximum(jnp.max(s, axis=-1, keepdims=True), sink)
        p = jnp.exp(s - m)
        denom = jnp.sum(p, axis=-1, keepdims=True) + jnp.exp(sink - m)
        p2 = jnp.concatenate([jnp.where(from_prev, p, 0.0), jnp.where(from_prev, 0.0, p)], axis=1).astype(BF16)
        o = jnp.dot(p2, vv, preferred_element_type=F32) / denom
        for g in range(ATT_GROUP):
            outs.append(o[g * ATT_BLOCK:(g + 1) * ATT_BLOCK])
    o_ref[...] = jnp.concatenate(outs, axis=1).astype(o_ref.dtype)


def _attention(proj, sink):
    l = proj.shape[0]
    kv_col = COL_KV // (2 * KV_WIDTH)
    return pl.pallas_call(
        _attn_kernel,
        grid=(l // ATT_BLOCK,),
        in_specs=[
            pl.BlockSpec(memory_space=pltpu.SMEM),
            pl.BlockSpec((ATT_BLOCK, ATT_WIDTH), lambda n: (n, COL_Q // ATT_WIDTH)),
            pl.BlockSpec((ATT_BLOCK, 2 * KV_WIDTH), lambda n: (n, kv_col)),
            pl.BlockSpec((ATT_BLOCK, 2 * KV_WIDTH), lambda n: (jnp.maximum(n - 1, 0), kv_col)),
        ],
        out_specs=pl.BlockSpec((ATT_BLOCK, ATT_WIDTH), lambda n: (n, 0)),
        out_shape=jax.ShapeDtypeStruct((l, ATT_WIDTH), BF16),
        compiler_params=_params(("arbitrary",)),
        name="attention",
    )(sink, proj, proj, proj)


def _pool_kernel(u0_ref, u1_ref, pw_ref, ps_ref, o_ref, ue_ref, *, tile):
    i = pl.program_id(0)

    @pl.when(i == 0)
    def _():
        ue_ref[0:POOL_HALO, :] = jnp.zeros((POOL_HALO, POOL_WIDTH), F32)

    ue_ref[POOL_HALO:POOL_HALO + tile, 0:HALF] = u0_ref[...].astype(F32)
    ue_ref[POOL_HALO:POOL_HALO + tile, HALF:2 * HALF] = u1_ref[...].astype(F32)
    t = i * tile + lax.broadcasted_iota(jnp.int32, (tile, 1), 0)
    for g, w in enumerate(POOL_WINDOWS):
        cols = slice(g * POOL_GROUP_DIM, (g + 1) * POOL_GROUP_DIM)
        cur = ue_ref[POOL_HALO:POOL_HALO + tile, cols]
        acc = cur
        for j in range(1, w):
            acc = acc + ue_ref[POOL_HALO - j:POOL_HALO - j + tile, cols]
        cnt = jnp.minimum(t + 1, w).astype(F32)
        mixed = (acc / cnt - cur).astype(BF16)
        y = jnp.dot(mixed, pw_ref[g], preferred_element_type=F32) * ps_ref[:, cols]
        o_ref[:, cols] = y.astype(o_ref.dtype)
    ue_ref[0:POOL_HALO, :] = ue_ref[tile:tile + POOL_HALO, :]


def _pool(proj, pool_w, pool_scale, tile):
    l = proj.shape[0]
    assert POOL_WIDTH == 2 * HALF
    return pl.pallas_call(
        functools.partial(_pool_kernel, tile=tile),
        grid=(l // tile,),
        in_specs=[
            pl.BlockSpec((tile, HALF), lambda i: (i, COL_U // HALF)),
            pl.BlockSpec((tile, HALF), lambda i: (i, COL_U // HALF + 1)),
            pl.BlockSpec((len(POOL_WINDOWS), POOL_GROUP_DIM, POOL_GROUP_DIM), lambda i: (0, 0, 0)),
            pl.BlockSpec((1, POOL_WIDTH), lambda i: (0, 0)),
        ],
        out_specs=pl.BlockSpec((tile, POOL_WIDTH), lambda i: (i, 0)),
        out_shape=jax.ShapeDtypeStruct((l, POOL_WIDTH), BF16),
        scratch_shapes=[pltpu.VMEM((tile + POOL_HALO, POOL_WIDTH), F32)],
        compiler_params=_params(("arbitrary",)),
        name="pool",
    )(proj, proj, pool_w, pool_scale)


def _split3(v):
    p1 = v.astype(BF16)
    r1 = v - p1.astype(F32)
    p2 = r1.astype(BF16)
    p3 = (r1 - p2.astype(F32)).astype(BF16)
    return jnp.concatenate([p1, p2, p3], axis=1)


def _head_expander():
    k = lax.broadcasted_iota(jnp.int32, (3 * DT_PAD, D_INNER), 0) % DT_PAD
    ch = lax.broadcasted_iota(jnp.int32, (3 * DT_PAD, D_INNER), 1) // SSM_HEAD_DIM
    return (k == ch).astype(BF16)


def _causal_conv_silu(ext_ref, cur, w_ref, b_ref):
    ext_ref[CONV_HALO:CONV_HALO + CHUNK, :] = cur
    acc = b_ref[...] + w_ref[CONV_K - 1:CONV_K, :] * cur
    for k in range(CONV_K - 1):
        off = CONV_HALO - (CONV_K - 1) + k
        acc = acc + w_ref[k:k + 1, :] * ext_ref[off:off + CHUNK, :]
    ext_ref[0:CONV_HALO, :] = ext_ref[CHUNK:CHUNK + CONV_HALO, :]
    return _silu(acc)


def _ssd_kernel(*refs):
    z_refs = refs[0:SSM_GROUPS]
    xs_refs = refs[SSM_GROUPS:2 * SSM_GROUPS]
    (b_ref, c_ref, dt_ref, cwx_ref, cbx_ref, cwbc_ref, cbbc_ref, dtb_ref, alog_ref, dskip_ref, nw_ref, ex_ref,
     o_ref, st_ref, xe_ref, bce_ref) = refs[2 * SSM_GROUPS:]
    c = pl.program_id(0)

    @pl.when(c == 0)
    def _():
        st_ref[...] = jnp.zeros_like(st_ref)
        xe_ref[0:CONV_HALO, :] = jnp.zeros((CONV_HALO, D_INNER), F32)
        bce_ref[0:CONV_HALO, :] = jnp.zeros((CONV_HALO, BC_WIDTH), F32)


    xs_in = jnp.concatenate([r[...] for r in xs_refs], axis=1).astype(F32)
    bc_in = jnp.concatenate([b_ref[...], c_ref[...]], axis=1).astype(F32)
    xs = _causal_conv_silu(xe_ref, xs_in, cwx_ref, cbx_ref)
    bcs = _causal_conv_silu(bce_ref, bc_in, cwbc_ref, cbbc_ref).astype(BF16)
    xs_b = xs.astype(BF16)

    dt = _softplus(dt_ref[...] + dtb_ref[...])
    da = dt * (-jnp.exp(alog_ref[...]))
    ri = lax.broadcasted_iota(jnp.int32, (CHUNK, CHUNK), 0)
    ci = lax.broadcasted_iota(jnp.int32, (CHUNK, CHUNK), 1)
    causal = ri >= ci
    a_cs = jnp.dot(causal.astype(F32), da, precision=lax.Precision.HIGHEST, preferred_element_type=F32)
    a_log2 = a_cs * LOG2_E
    a_log2_t = a_log2.T
    dt_t = dt.T
    ecs = jnp.exp(a_cs)
    a_last = a_cs[CHUNK - 1:CHUNK, :]
    per_head = jnp.concatenate([dt * jnp.exp(a_last - a_cs), jnp.broadcast_to(jnp.exp(a_last), (8, DT_PAD))], axis=0)
    per_ch = jnp.dot(_split3(per_head), ex_ref[...], preferred_element_type=F32)
    xdte_b = (xs * per_ch[0:CHUNK]).astype(BF16)
    chunk_decay = per_ch[CHUNK:CHUNK + 1]

    pr = lax.broadcasted_iota(jnp.int32, (4 * CHUNK, 2 * SSM_HEAD_DIM), 0) < 2 * CHUNK
    pc = lax.broadcasted_iota(jnp.int32, (4 * CHUNK, 2 * SSM_HEAD_DIM), 1) < SSM_HEAD_DIM
    pair_mask = pr == pc

    for g in range(SSM_GROUPS):
        gsl = slice(g * GROUP_CH, (g + 1) * GROUP_CH)
        b_g = bcs[:, g * D_STATE:(g + 1) * D_STATE]
        c_g = bcs[:, (SSM_GROUPS + g) * D_STATE:(SSM_GROUPS + g + 1) * D_STATE]
        cb = lax.dot_general(c_g, b_g, (((1,), (1,)), ((), ())), preferred_element_type=F32)
        cb = jnp.where(causal, cb, 0.0)
        c_f = c_g.astype(F32)
        s_prev = st_ref[g]
        s_prev_b = s_prev.astype(BF16)
        tiles = []
        for p in range(HEADS_PER_GROUP // 2):
            parts = []
            for h in (g * HEADS_PER_GROUP + 2 * p, g * HEADS_PER_GROUP + 2 * p + 1):
                a_col = jnp.broadcast_to(a_log2[:, h:h + 1], (CHUNK, CHUNK))
                decay = jnp.exp2(jnp.minimum(a_col - a_log2_t[h:h + 1, :], 0.0))
                parts.append(cb * decay * dt_t[h:h + 1, :])
                parts.append(c_f * jnp.broadcast_to(ecs[:, h:h + 1], (CHUNK, CHUNK)))
            lhs = jnp.concatenate(parts, axis=1).astype(BF16)
            psl = slice(p * 2 * SSM_HEAD_DIM, (p + 1) * 2 * SSM_HEAD_DIM)
            x_p = xs_b[:, g * GROUP_CH + p * 2 * SSM_HEAD_DIM:g * GROUP_CH + (p + 1) * 2 * SSM_HEAD_DIM]
            s_p = s_prev_b[:, psl]
            rhs = jnp.concatenate([x_p, s_p, x_p, s_p], axis=0)
            rhs = jnp.where(pair_mask, rhs, jnp.zeros_like(rhs))
            tiles.append(jnp.dot(lhs, rhs, preferred_element_type=F32))
        s_new = lax.dot_general(b_g, xdte_b[:, gsl], (((0,), (0,)), ((), ())), preferred_element_type=F32)
        st_ref[g] = s_prev * chunk_decay[:, gsl] + s_new
        y = jnp.concatenate(tiles, axis=1) + dskip_ref[:, gsl] * xs[:, gsl]
        gated = y * _silu(z_refs[g][...].astype(F32))
        o_ref[:, gsl] = _rmsnorm_rows(gated, nw_ref[:, gsl]).astype(o_ref.dtype)


def _ssd(proj, dt_raw, cwx, cbx, cwbc, cbbc, dtb, alog, dskip, nw):
    l = proj.shape[0]
    assert GROUP_CH == HALF and BC_WIDTH == 2 * HALF
    full = lambda shape: pl.BlockSpec(shape, lambda c: (0, 0))
    col = lambda off: pl.BlockSpec((CHUNK, HALF), lambda c: (c, off // HALF))
    n_proj = 2 * SSM_GROUPS + 2
    return pl.pallas_call(
        _ssd_kernel,
        grid=(l // CHUNK,),
        in_specs=(
            [col(COL_Z + g * HALF) for g in range(SSM_GROUPS)]
            + [col(COL_XS + g * HALF) for g in range(SSM_GROUPS)]
            + [col(COL_BC), col(COL_BC + HALF)]
            + [pl.BlockSpec((CHUNK, DT_PAD), lambda c: (c, 0)),
               full((CONV_K, D_INNER)), full((1, D_INNER)),
               full((CONV_K, BC_WIDTH)), full((1, BC_WIDTH)),
               full((1, DT_PAD)), full((1, DT_PAD)),
               full((1, D_INNER)), full((1, D_INNER)),
               full((3 * DT_PAD, D_INNER))]),
        out_specs=pl.BlockSpec((CHUNK, D_INNER), lambda c: (c, 0)),
        out_shape=jax.ShapeDtypeStruct((l, D_INNER), BF16),
        scratch_shapes=[
            pltpu.VMEM((SSM_GROUPS, D_STATE, GROUP_CH), F32),
            pltpu.VMEM((CHUNK + CONV_HALO, D_INNER), F32),
            pltpu.VMEM((CHUNK + CONV_HALO, BC_WIDTH), F32),
        ],
        compiler_params=_params(("arbitrary",)),
        name="ssd",
    )(*([proj] * n_proj), dt_raw, cwx, cbx, cwbc, cbbc, dtb, alog, dskip, nw, _head_expander())


def _merge_kernel(att_ref, pool_ref, ssm_ref, g0_ref, g1_ref, g2_ref, wa_ref, wp_ref, ws_ref, o_ref):
    def branch(a_ref, w_ref, g_ref):
        return jax.nn.sigmoid(g_ref[...].astype(F32)) * jnp.dot(a_ref[...], w_ref[...], preferred_element_type=F32)

    merged = branch(att_ref, wa_ref, g0_ref) + branch(pool_ref, wp_ref, g1_ref) + branch(ssm_ref, ws_ref, g2_ref)
    o_ref[...] = merged.astype(o_ref.dtype)


def _merge(att, pool, ssm, proj, wa, wp, ws, tm, tn):
    l = att.shape[0]
    assert D_MODEL % tn == 0 and COL_GATES % tn == 0 and l % tm == 0
    gate_spec = lambda b: pl.BlockSpec((tm, tn), lambda i, j: (i, (COL_GATES + b * D_MODEL) // tn + j))
    return pl.pallas_call(
        _merge_kernel,
        grid=(l // tm, D_MODEL // tn),
        in_specs=[
            pl.BlockSpec((tm, ATT_WIDTH), lambda i, j: (i, 0)),
            pl.BlockSpec((tm, POOL_WIDTH), lambda i, j: (i, 0)),
            pl.BlockSpec((tm, D_INNER), lambda i, j: (i, 0)),
            gate_spec(0), gate_spec(1), gate_spec(2),
            pl.BlockSpec((ATT_WIDTH, tn), lambda i, j: (0, j)),
            pl.BlockSpec((POOL_WIDTH, tn), lambda i, j: (0, j)),
            pl.BlockSpec((D_INNER, tn), lambda i, j: (0, j)),
        ],
        out_specs=pl.BlockSpec((tm, tn), lambda i, j: (i, j)),
        out_shape=jax.ShapeDtypeStruct((l, D_MODEL), BF16),
        compiler_params=_params(("arbitrary", "arbitrary")),
        name="merge",
    )(att, pool, ssm, proj, proj, proj, wa, wp, ws)


def _outproj_kernel(m_ref, w_ref, x_ref, lnw_ref, o_ref, h_ref):
    y = x_ref[...] + jnp.dot(m_ref[...], w_ref[...], preferred_element_type=F32)
    o_ref[...] = y
    h_ref[...] = _rmsnorm_rows(y, lnw_ref[...]).astype(h_ref.dtype)


def _outproj(merged, w_out, x, lnw, tm):
    l = x.shape[0]
    row_tile = pl.BlockSpec((tm, D_MODEL), lambda i: (i, 0))
    return pl.pallas_call(
        _outproj_kernel,
        grid=(l // tm,),
        in_specs=[
            row_tile,
            pl.BlockSpec((D_MODEL, D_MODEL), lambda i: (0, 0)),
            row_tile,
            pl.BlockSpec((1, D_MODEL), lambda i: (0, 0)),
        ],
        out_specs=[row_tile, row_tile],
        out_shape=[jax.ShapeDtypeStruct((l, D_MODEL), F32), jax.ShapeDtypeStruct((l, D_MODEL), BF16)],
        compiler_params=_params(("arbitrary",)),
        name="outproj",
    )(merged, w_out, x, lnw)


def _ffn_up_kernel(h_ref, wg_ref, wu_ref, o_ref, wgb_ref, wub_ref):
    _cast_at_first_row_tile(wg_ref, wgb_ref)
    _cast_at_first_row_tile(wu_ref, wub_ref)
    h = h_ref[...]
    gate = jnp.dot(h, wgb_ref[...], preferred_element_type=F32)
    up = jnp.dot(h, wub_ref[...], preferred_element_type=F32)
    o_ref[...] = (_silu(gate) * up).astype(o_ref.dtype)


def _ffn_up(h, w_gate_up, layer, tm, tf):
    l = h.shape[0]
    assert D_FF % tf == 0 and l % tm == 0
    nf = D_FF // tf
    return pl.pallas_call(
        _ffn_up_kernel,
        grid=(nf, l // tm),
        in_specs=[
            pl.BlockSpec((tm, D_MODEL), lambda j, i: (i, 0)),
            pl.BlockSpec((None, D_MODEL, tf), lambda j, i: (layer, 0, j)),
            pl.BlockSpec((None, D_MODEL, tf), lambda j, i: (layer, 0, nf + j)),
        ],
        out_specs=pl.BlockSpec((tm, tf), lambda j, i: (i, j)),
        out_shape=jax.ShapeDtypeStruct((l, D_FF), BF16),
        scratch_shapes=[pltpu.VMEM((D_MODEL, tf), BF16), pltpu.VMEM((D_MODEL, tf), BF16)],
        compiler_params=_params(("arbitrary", "arbitrary")),
        name="ffn_up",
    )(h, w_gate_up, w_gate_up)


def _ffn_down_kernel(a_ref, w_ref, x_ref, o_ref, wb_ref):
    _cast_at_first_row_tile(w_ref, wb_ref)
    o_ref[...] = x_ref[...] + jnp.dot(a_ref[...], wb_ref[...], preferred_element_type=F32)


def _ffn_down(act, w_down, layer, x, tm, tn):
    l = x.shape[0]
    assert D_MODEL % tn == 0 and l % tm == 0
    return pl.pallas_call(
        _ffn_down_kernel,
        grid=(D_MODEL // tn, l // tm),
        in_specs=[
            pl.BlockSpec((tm, D_FF), lambda j, i: (i, 0)),
            pl.BlockSpec((None, D_FF, tn), lambda j, i: (layer, 0, j)),
            pl.BlockSpec((tm, tn), lambda j, i: (i, j)),
        ],
        out_specs=pl.BlockSpec((tm, tn), lambda j, i: (i, j)),
        out_shape=jax.ShapeDtypeStruct((l, D_MODEL), F32),
        scratch_shapes=[pltpu.VMEM((D_FF, tn), BF16)],
        compiler_params=_params(("arbitrary", "arbitrary")),
        name="ffn_down",
    )(act, w_down, x)


def _pad_heads(v):
    return jnp.pad(v, (0, DT_PAD - SSM_HEADS)).reshape(1, DT_PAD)


def _dt_weight(w_in, layer):
    return jnp.pad(w_in[layer, :, W_COL_DT:W_COL_GATES], ((0, 0), (0, DT_PAD - SSM_HEADS))).astype(BF16)


def _proj_weight(w_in, layer):
    return jnp.concatenate([w_in[layer, :, W_COL_GATES:], w_in[layer, :, :W_MAIN_COLS]], axis=1).astype(BF16)


def kernel(x, ln1_w, w_in, attn_sink, conv_w, conv_b, dt_bias, a_log, d_skip, ssm_norm_w, pool_w, pool_scale,
           w_attn_br, w_pool_br, w_ssm_br, w_out, ln2_w, w_gate_up, w_down, final_w):
    b, l, _ = x.shape
    depth = w_in.shape[0]
    assert l % CHUNK == 0 and CHUNK == ATT_BLOCK
    tm = min(512, l)
    tm_big = min(1024, l)
    row = lambda v: v.reshape(1, -1)
    outs = []
    for bi in range(b):
        xb = x[bi]
        for i in range(depth):
            proj, dt_raw = _inproj(xb, row(ln1_w[i]), _proj_weight(w_in, i), _dt_weight(w_in, i), tm_big, 1536)
            att = _attention(proj, attn_sink[i])
            pool = _pool(proj, pool_w[i].astype(BF16), row(pool_scale[i]), tm)
            ssm = _ssd(proj, dt_raw,
                       conv_w[i][:, :D_INNER], row(conv_b[i][:D_INNER]),
                       conv_w[i][:, D_INNER:], row(conv_b[i][D_INNER:]),
                       _pad_heads(dt_bias[i]), _pad_heads(a_log[i]),
                       row(jnp.repeat(d_skip[i], SSM_HEAD_DIM)), row(ssm_norm_w[i]))
            merged = _merge(att, pool, ssm, proj, w_attn_br[i].astype(BF16), w_pool_br[i].astype(BF16),
                            w_ssm_br[i].astype(BF16), tm_big, 1024)
            xb, h2 = _outproj(merged, w_out[i].astype(BF16), xb, row(ln2_w[i]), tm)
            act = _ffn_up(h2, w_gate_up, i, tm_big, 512)
            xb = _ffn_down(act, w_down, i, xb, tm, 512)
        outs.append(_final_norm(xb, row(final_w), tm))
    return jnp.stack(outs, axis=0)
```

```python
import functools

import jax
import jax.numpy as jnp
from jax import lax
from jax.experimental import pallas as pl
from jax.experimental.pallas import tpu as pltpu

F32 = jnp.float32
BF16 = jnp.bfloat16

D_MODEL = 2048
ATT_HEAD_DIM = 64
ATT_Q_HEADS = 16
ATT_KV_HEADS = 4
ATT_GROUP = ATT_Q_HEADS // ATT_KV_HEADS
ATT_BLOCK = 128
ATT_BLOCKS_PER_STEP = 2
ATT_WIDTH = ATT_Q_HEADS * ATT_HEAD_DIM
KV_WIDTH = ATT_KV_HEADS * ATT_HEAD_DIM
POOL_WINDOWS = (2, 4, 8, 16)
POOL_WIDTH = D_MODEL // 2
POOL_GROUP_DIM = POOL_WIDTH // len(POOL_WINDOWS)
POOL_HALO = 16
D_INNER = D_MODEL
SSM_HEAD_DIM = 64
SSM_HEADS = D_INNER // SSM_HEAD_DIM
SSM_GROUPS = 4
HEADS_PER_GROUP = SSM_HEADS // SSM_GROUPS
GROUP_CH = D_INNER // SSM_GROUPS
D_STATE = 128
CONV_K = 4
CONV_HALO = 8
CHUNK = 128
SSD_CHUNKS_PER_STEP = 2
BC_WIDTH = 2 * SSM_GROUPS * D_STATE
N_BRANCH = 3
D_FF = 5632
EPS = 1e-6
DT_PAD = 128
MASK_VALUE = -1e30
LOG2_E = 1.4426950408889634

W_MAIN_COLS = 7680
W_COL_DT = W_MAIN_COLS
W_COL_GATES = W_COL_DT + SSM_HEADS
GATE_COLS = N_BRANCH * D_MODEL
COL_GATES = 0
COL_Q = GATE_COLS
COL_KV = COL_Q + ATT_WIDTH
COL_U = COL_KV + 2 * KV_WIDTH
COL_Z = COL_U + POOL_WIDTH
COL_XS = COL_Z + D_INNER
COL_BC = COL_XS + D_INNER
PROJ_COLS = COL_BC + BC_WIDTH
HALF = 512

VMEM_LIMIT_BYTES = 56 * 1024 * 1024


def _params(sem):
    return pltpu.CompilerParams(dimension_semantics=sem, vmem_limit_bytes=VMEM_LIMIT_BYTES)


def _silu(v):
    h = 0.5 * v
    return h + h * jnp.tanh(h)


def _softplus(v):
    return jnp.maximum(v, 0.0) + jnp.log1p(jnp.exp(-jnp.abs(v)))


def _rmsnorm_rows(x, w):
    ms = jnp.mean(x * x, axis=-1, keepdims=True)
    return x * lax.rsqrt(ms + EPS) * w


def _inproj_kernel(x_ref, lnw_ref, w_ref, wdt_ref, o_ref, dt_ref, h_ref):
    @pl.when(pl.program_id(1) == 0)
    def _():
        h = _rmsnorm_rows(x_ref[...], lnw_ref[...]).astype(BF16)
        h_ref[...] = h
        dt_ref[...] = jnp.dot(h, wdt_ref[...], preferred_element_type=F32)

    o_ref[...] = jnp.dot(h_ref[...], w_ref[...], preferred_element_type=F32).astype(o_ref.dtype)


def _inproj(x, lnw, w, wdt, tm, tn):
    l = x.shape[0]
    assert PROJ_COLS % tn == 0 and l % tm == 0
    return pl.pallas_call(
        _inproj_kernel,
        grid=(l // tm, PROJ_COLS // tn),
        in_specs=[
            pl.BlockSpec((tm, D_MODEL), lambda i, j: (i, 0)),
            pl.BlockSpec((1, D_MODEL), lambda i, j: (0, 0)),
            pl.BlockSpec((D_MODEL, tn), lambda i, j: (0, j)),
            pl.BlockSpec((D_MODEL, DT_PAD), lambda i, j: (0, 0)),
        ],
        out_specs=[
            pl.BlockSpec((tm, tn), lambda i, j: (i, j)),
            pl.BlockSpec((tm, DT_PAD), lambda i, j: (i, 0)),
        ],
        out_shape=[
            jax.ShapeDtypeStruct((l, PROJ_COLS), BF16),
            jax.ShapeDtypeStruct((l, DT_PAD), F32),
        ],
        scratch_shapes=[pltpu.VMEM((tm, D_MODEL), BF16)],
        compiler_params=_params(("arbitrary", "arbitrary")),
        name="inproj",
    )(x, lnw, w, wdt)


def _final_norm_kernel(x_ref, w_ref, o_ref):
    o_ref[...] = _rmsnorm_rows(x_ref[...], w_ref[...])


def _final_norm(x, w, tm):
    l = x.shape[0]
    return pl.pallas_call(
        _final_norm_kernel,
        grid=(l // tm,),
        in_specs=[pl.BlockSpec((tm, D_MODEL), lambda i: (i, 0)), pl.BlockSpec((1, D_MODEL), lambda i: (0, 0))],
        out_specs=pl.BlockSpec((tm, D_MODEL), lambda i: (i, 0)),
        out_shape=jax.ShapeDtypeStruct((l, D_MODEL), F32),
        compiler_params=_params(("arbitrary",)),
        name="final_norm",
    )(x, w)


def _cast_at_first_row_tile(w_ref, wb_ref):
    @pl.when(pl.program_id(1) == 0)
    def _():
        wb_ref[...] = w_ref[...].astype(BF16)


def _attn_kernel(sink_ref, q_ref, kv_ref, kvp_ref, o_ref):
    n = pl.program_id(0)
    rows = ATT_GROUP * ATT_BLOCK
    qi = lax.broadcasted_iota(jnp.int32, (rows, ATT_BLOCK), 0) % ATT_BLOCK
    kj = lax.broadcasted_iota(jnp.int32, (rows, ATT_BLOCK), 1)
    from_prev = kj > qi
    row_head = lax.broadcasted_iota(jnp.int32, (rows, 1), 0) // ATT_BLOCK
    q_all = q_ref[...] * (ATT_HEAD_DIM ** -0.5)
    kv_all = kv_ref[...]

    for blk in range(ATT_BLOCKS_PER_STEP):
        rsl = slice(blk * ATT_BLOCK, (blk + 1) * ATT_BLOCK)
        q = q_all[rsl]
        kv_cur = kv_all[rsl]
        if blk == 0:
            kv_prev = kvp_ref[...]
            prev_bias = jnp.where(n > 0, 0.0, MASK_VALUE)
        else:
            kv_prev = kv_all[(blk - 1) * ATT_BLOCK:blk * ATT_BLOCK]
            prev_bias = 0.0
        outs = []
        for kh in range(ATT_KV_HEADS):
            ksl = slice(kh * ATT_HEAD_DIM, (kh + 1) * ATT_HEAD_DIM)
            vsl = slice(KV_WIDTH + kh * ATT_HEAD_DIM, KV_WIDTH + (kh + 1) * ATT_HEAD_DIM)
            kk = jnp.concatenate([kv_prev[:, ksl], kv_cur[:, ksl]], axis=0)
            vv = jnp.concatenate([kv_prev[:, vsl], kv_cur[:, vsl]], axis=0)
            qs = jnp.concatenate(
                [q[:, (kh * ATT_GROUP + g) * ATT_HEAD_DIM:(kh * ATT_GROUP + g + 1) * ATT_HEAD_DIM]
                 for g in range(ATT_GROUP)], axis=0)
            s = lax.dot_general(qs, kk, (((1,), (1,)), ((), ())), preferred_element_type=F32)
            s = jnp.where(from_prev, s[:, :ATT_BLOCK] + prev_bias, s[:, ATT_BLOCK:])
            sink = jnp.zeros((rows, 1), F32)
            for g in range(ATT_GROUP):
                sink = jnp.where(row_head == g, sink_ref[kh * ATT_GROUP + g], sink)
            m = jnp.maximum(jnp.max(s, axis=-1, keepdims=True), sink)
            p = jnp.exp(s - m)
            denom = jnp.sum(p, axis=-1, keepdims=True) + jnp.exp(sink - m)
            p2 = jnp.concatenate([jnp.where(from_prev, p, 0.0), jnp.where(from_prev, 0.0, p)], axis=1).astype(BF16)
            o = jnp.dot(p2, vv, preferred_element_type=F32) / denom
            for g in range(ATT_GROUP):
                outs.append(o[g * ATT_BLOCK:(g + 1) * ATT_BLOCK])
        o_ref[rsl, :] = jnp.concatenate(outs, axis=1).astype(o_ref.dtype)


def _attention(proj, sink):
    l = proj.shape[0]
    kv_col = COL_KV // (2 * KV_WIDTH)
    rows = ATT_BLOCKS_PER_STEP * ATT_BLOCK
    assert l % rows == 0
    return pl.pallas_call(
        _attn_kernel,
        grid=(l // rows,),
        in_specs=[
            pl.BlockSpec(memory_space=pltpu.SMEM),
            pl.BlockSpec((rows, ATT_WIDTH), lambda n: (n, COL_Q // ATT_WIDTH)),
            pl.BlockSpec((rows, 2 * KV_WIDTH), lambda n: (n, kv_col)),
            pl.BlockSpec((ATT_BLOCK, 2 * KV_WIDTH), lambda n: (jnp.maximum(n * ATT_BLOCKS_PER_STEP - 1, 0), kv_col)),
        ],
        out_specs=pl.BlockSpec((rows, ATT_WIDTH), lambda n: (n, 0)),
        out_shape=jax.ShapeDtypeStruct((l, ATT_WIDTH), BF16),
        compiler_params=_params(("arbitrary",)),
        name="attention",
    )(sink, proj, proj, proj)


def _pool_kernel(u0_ref, u1_ref, pw_ref, ps_ref, o_ref, ue_ref, *, tile):
    i = pl.program_id(0)

    @pl.when(i == 0)
    def _():
        ue_ref[0:POOL_HALO, :] = jnp.zeros((POOL_HALO, POOL_WIDTH), F32)

    ue_ref[POOL_HALO:POOL_HALO + tile, 0:HALF] = u0_ref[...].astype(F32)
    ue_ref[POOL_HALO:POOL_HALO + tile, HALF:2 * HALF] = u1_ref[...].astype(F32)
    t = i * tile + lax.broadcasted_iota(jnp.int32, (tile, 1), 0)
    for g, w in enumerate(POOL_WINDOWS):
        cols = slice(g * POOL_GROUP_DIM, (g + 1) * POOL_GROUP_DIM)
        cur = ue_ref[POOL_HALO:POOL_HALO + tile, cols]
        acc = cur
        for j in range(1, w):
            acc = acc + ue_ref[POOL_HALO - j:POOL_HALO - j + tile, cols]
        cnt = jnp.minimum(t + 1, w).astype(F32)
        mixed = (acc / cnt - cur).astype(BF16)
        y = jnp.dot(mixed, pw_ref[g], preferred_element_type=F32) * ps_ref[:, cols]
        o_ref[:, cols] = y.astype(o_ref.dtype)
    ue_ref[0:POOL_HALO, :] = ue_ref[tile:tile + POOL_HALO, :]


def _pool(proj, pool_w, pool_scale, tile):
    l = proj.shape[0]
    assert POOL_WIDTH == 2 * HALF
    return pl.pallas_call(
        functools.partial(_pool_kernel, tile=tile),
        grid=(l // tile,),
        in_specs=[
            pl.BlockSpec((tile, HALF), lambda i: (i, COL_U // HALF)),
            pl.BlockSpec((tile, HALF), lambda i: (i, COL_U // HALF + 1)),
            pl.BlockSpec((len(POOL_WINDOWS), POOL_GROUP_DIM, POOL_GROUP_DIM), lambda i: (0, 0, 0)),
            pl.BlockSpec((1, POOL_WIDTH), lambda i: (0, 0)),
        ],
        out_specs=pl.BlockSpec((tile, POOL_WIDTH), lambda i: (i, 0)),
        out_shape=jax.ShapeDtypeStruct((l, POOL_WIDTH), BF16),
        scratch_shapes=[pltpu.VMEM((tile + POOL_HALO, POOL_WIDTH), F32)],
        compiler_params=_params(("arbitrary",)),
        name="pool",
    )(proj, proj, pool_w, pool_scale)


def _split3(v):
    p1 = v.astype(BF16)
    r1 = v - p1.astype(F32)
    p2 = r1.astype(BF16)
    p3 = (r1 - p2.astype(F32)).astype(BF16)
    return jnp.concatenate([p1, p2, p3], axis=1)


def _head_expander():
    k = lax.broadcasted_iota(jnp.int32, (3 * DT_PAD, D_INNER), 0) % DT_PAD
    ch = lax.broadcasted_iota(jnp.int32, (3 * DT_PAD, D_INNER), 1) // SSM_HEAD_DIM
    return (k == ch).astype(BF16)


def _causal_conv_silu(ext_ref, cur, w_ref, b_ref):
    ext_ref[CONV_HALO:CONV_HALO + CHUNK, :] = cur
    acc = b_ref[...] + w_ref[CONV_K - 1:CONV_K, :] * cur
    for k in range(CONV_K - 1):
        off = CONV_HALO - (CONV_K - 1) + k
        acc = acc + w_ref[k:k + 1, :] * ext_ref[off:off + CHUNK, :]
    ext_ref[0:CONV_HALO, :] = ext_ref[CHUNK:CHUNK + CONV_HALO, :]
    return _silu(acc)


def _ssd_kernel(*refs):
    z_refs = refs[0:SSM_GROUPS]
    xs_refs = refs[SSM_GROUPS:2 * SSM_GROUPS]
    (b_ref, c_ref, dt_ref, cwx_ref, cbx_ref, cwbc_ref, cbbc_ref, dtb_ref, alog_ref, dskip_ref, nw_ref, ex_ref,
     o_ref, st_ref, xe_ref, bce_ref) = refs[2 * SSM_GROUPS:]
    c = pl.program_id(0)

    @pl.when(c == 0)
    def _():
        st_ref[...] = jnp.zeros_like(st_ref)
        xe_ref[0:CONV_HALO, :] = jnp.zeros((CONV_HALO, D_INNER), F32)
        bce_ref[0:CONV_HALO, :] = jnp.zeros((CONV_HALO, BC_WIDTH), F32)

    for sub in range(SSD_CHUNKS_PER_STEP):
        _ssd_chunk(slice(sub * CHUNK, (sub + 1) * CHUNK), z_refs, xs_refs, b_ref, c_ref, dt_ref, cwx_ref, cbx_ref,
                   cwbc_ref, cbbc_ref, dtb_ref, alog_ref, dskip_ref, nw_ref, ex_ref, o_ref, st_ref, xe_ref, bce_ref)


def _ssd_chunk(rows, z_refs, xs_refs, b_ref, c_ref, dt_ref, cwx_ref, cbx_ref, cwbc_ref, cbbc_ref, dtb_ref, alog_ref,
               dskip_ref, nw_ref, ex_ref, o_ref, st_ref, xe_ref, bce_ref):
    xs_in = jnp.concatenate([r[rows, :] for r in xs_refs], axis=1).astype(F32)
    bc_in = jnp.concatenate([b_ref[rows, :], c_ref[rows, :]], axis=1).astype(F32)
    xs = _causal_conv_silu(xe_ref, xs_in, cwx_ref, cbx_ref)
    bcs = _causal_conv_silu(bce_ref, bc_in, cwbc_ref, cbbc_ref).astype(BF16)
    xs_b = xs.astype(BF16)

    dt = _softplus(dt_ref[rows, :] + dtb_ref[...])
    da = dt * (-jnp.exp(alog_ref[...]))
    ri = lax.broadcasted_iota(jnp.int32, (CHUNK, CHUNK), 0)
    ci = lax.broadcasted_iota(jnp.int32, (CHUNK, CHUNK), 1)
    causal = ri >= ci
    a_cs = jnp.dot(causal.astype(F32), da, precision=lax.Precision.HIGHEST, preferred_element_type=F32)
    a_log2 = a_cs * LOG2_E
    a_log2_t = a_log2.T
    dt_t = dt.T
    ecs = jnp.exp(a_cs)
    a_last = a_cs[CHUNK - 1:CHUNK, :]
    per_head = jnp.concatenate([dt * jnp.exp(a_last - a_cs), jnp.broadcast_to(jnp.exp(a_last), (8, DT_PAD))], axis=0)
    per_ch = jnp.dot(_split3(per_head), ex_ref[...], preferred_element_type=F32)
    xdte_b = (xs * per_ch[0:CHUNK]).astype(BF16)
    chunk_decay = per_ch[CHUNK:CHUNK + 1]

    pr = lax.broadcasted_iota(jnp.int32, (4 * CHUNK, 2 * SSM_HEAD_DIM), 0) < 2 * CHUNK
    pc = lax.broadcasted_iota(jnp.int32, (4 * CHUNK, 2 * SSM_HEAD_DIM), 1) < SSM_HEAD_DIM
    pair_mask = pr == pc

    for g in range(SSM_GROUPS):
        gsl = slice(g * GROUP_CH, (g + 1) * GROUP_CH)
        b_g = bcs[:, g * D_STATE:(g + 1) * D_STATE]
        c_g = bcs[:, (SSM_GROUPS + g) * D_STATE:(SSM_GROUPS + g + 1) * D_STATE]
        cb = lax.dot_general(c_g, b_g, (((1,), (1,)), ((), ())), preferred_element_type=F32)
        cb = jnp.where(causal, cb, 0.0)
        c_f = c_g.astype(F32)
        s_prev = st_ref[g]
        s_prev_b = s_prev.astype(BF16)
        tiles = []
        for p in range(HEADS_PER_GROUP // 2):
            parts = []
            for h in (g * HEADS_PER_GROUP + 2 * p, g * HEADS_PER_GROUP + 2 * p + 1):
                a_col = jnp.broadcast_to(a_log2[:, h:h + 1], (CHUNK, CHUNK))
                decay = jnp.exp2(jnp.minimum(a_col - a_log2_t[h:h + 1, :], 0.0))
                parts.append(cb * decay * dt_t[h:h + 1, :])
                parts.append(c_f * jnp.broadcast_to(ecs[:, h:h + 1], (CHUNK, CHUNK)))
            lhs = jnp.concatenate(parts, axis=1).astype(BF16)
            psl = slice(p * 2 * SSM_HEAD_DIM, (p + 1) * 2 * SSM_HEAD_DIM)
            x_p = xs_b[:, g * GROUP_CH + p * 2 * SSM_HEAD_DIM:g * GROUP_CH + (p + 1) * 2 * SSM_HEAD_DIM]
            s_p = s_prev_b[:, psl]
            rhs = jnp.concatenate([x_p, s_p, x_p, s_p], axis=0)
            rhs = jnp.where(pair_mask, rhs, jnp.zeros_like(rhs))
            tiles.append(jnp.dot(lhs, rhs, preferred_element_type=F32))
        s_new = lax.dot_general(b_g, xdte_b[:, gsl], (((0,), (0,)), ((), ())), preferred_element_type=F32)
        st_ref[g] = s_prev * chunk_decay[:, gsl] + s_new
        y = jnp.concatenate(tiles, axis=1) + dskip_ref[:, gsl] * xs[:, gsl]
        gated = y * _silu(z_refs[g][rows, :].astype(F32))
        o_ref[rows, gsl] = _rmsnorm_rows(gated, nw_ref[:, gsl]).astype(o_ref.dtype)


def _ssd(proj, dt_raw, cwx, cbx, cwbc, cbbc, dtb, alog, dskip, nw):
    l = proj.shape[0]
    assert GROUP_CH == HALF and BC_WIDTH == 2 * HALF
    rows = SSD_CHUNKS_PER_STEP * CHUNK
    assert l % rows == 0
    full = lambda shape: pl.BlockSpec(shape, lambda c: (0, 0))
    col = lambda off: pl.BlockSpec((rows, HALF), lambda c: (c, off // HALF))
    n_proj = 2 * SSM_GROUPS + 2
    return pl.pallas_call(
        _ssd_kernel,
        grid=(l // rows,),
        in_specs=(
            [col(COL_Z + g * HALF) for g in range(SSM_GROUPS)]
            + [col(COL_XS + g * HALF) for g in range(SSM_GROUPS)]
            + [col(COL_BC), col(COL_BC + HALF)]
            + [pl.BlockSpec((rows, DT_PAD), lambda c: (c, 0)),
               full((CONV_K, D_INNER)), full((1, D_INNER)),
               full((CONV_K, BC_WIDTH)), full((1, BC_WIDTH)),
               full((1, DT_PAD)), full((1, DT_PAD)),
               full((1, D_INNER)), full((1, D_INNER)),
               full((3 * DT_PAD, D_INNER))]),
        out_specs=pl.BlockSpec((rows, D_INNER), lambda c: (c, 0)),
        out_shape=jax.ShapeDtypeStruct((l, D_INNER), BF16),
        scratch_shapes=[
            pltpu.VMEM((SSM_GROUPS, D_STATE, GROUP_CH), F32),
            pltpu.VMEM((CHUNK + CONV_HALO, D_INNER), F32),
            pltpu.VMEM((CHUNK + CONV_HALO, BC_WIDTH), F32),
        ],
        compiler_params=_params(("arbitrary",)),
        name="ssd",
    )(*([proj] * n_proj), dt_raw, cwx, cbx, cwbc, cbbc, dtb, alog, dskip, nw, _head_expander())


def _merge_kernel(att_ref, pool_ref, ssm_ref, g0_ref, g1_ref, g2_ref, wa_ref, wp_ref, ws_ref, o_ref):
    def branch(a_ref, w_ref, g_ref):
        return jax.nn.sigmoid(g_ref[...].astype(F32)) * jnp.dot(a_ref[...], w_ref[...], preferred_element_type=F32)

    merged = branch(att_ref, wa_ref, g0_ref) + branch(pool_ref, wp_ref, g1_ref) + branch(ssm_ref, ws_ref, g2_ref)
    o_ref[...] = merged.astype(o_ref.dtype)


def _merge(att, pool, ssm, proj, wa, wp, ws, tm, tn):
    l = att.shape[0]
    assert D_MODEL % tn == 0 and COL_GATES % tn == 0 and l % tm == 0
    gate_spec = lambda b: pl.BlockSpec((tm, tn), lambda i, j: (i, (COL_GATES + b * D_MODEL) // tn + j))
    return pl.pallas_call(
        _merge_kernel,
        grid=(l // tm, D_MODEL // tn),
        in_specs=[
            pl.BlockSpec((tm, ATT_WIDTH), lambda i, j: (i, 0)),
            pl.BlockSpec((tm, POOL_WIDTH), lambda i, j: (i, 0)),
            pl.BlockSpec((tm, D_INNER), lambda i, j: (i, 0)),
            gate_spec(0), gate_spec(1), gate_spec(2),
            pl.BlockSpec((ATT_WIDTH, tn), lambda i, j: (0, j)),
            pl.BlockSpec((POOL_WIDTH, tn), lambda i, j: (0, j)),
            pl.BlockSpec((D_INNER, tn), lambda i, j: (0, j)),
        ],
        out_specs=pl.BlockSpec((tm, tn), lambda i, j: (i, j)),
        out_shape=jax.ShapeDtypeStruct((l, D_MODEL), BF16),
        compiler_params=_params(("arbitrary", "arbitrary")),
        name="merge",
    )(att, pool, ssm, proj, proj, proj, wa, wp, ws)


def _outproj_kernel(m_ref, w_ref, x_ref, lnw_ref, o_ref, h_ref):
    y = x_ref[...] + jnp.dot(m_ref[...], w_ref[...], preferred_element_type=F32)
    o_ref[...] = y
    h_ref[...] = _rmsnorm_rows(y, lnw_ref[...]).astype(h_ref.dtype)


def _outproj(merged, w_out, x, lnw, tm):
    l = x.shape[0]
    row_tile = pl.BlockSpec((tm, D_MODEL), lambda i: (i, 0))
    return pl.pallas_call(
        _outproj_kernel,
        grid=(l // tm,),
        in_specs=[
            row_tile,
            pl.BlockSpec((D_MODEL, D_MODEL), lambda i: (0, 0)),
            row_tile,
            pl.BlockSpec((1, D_MODEL), lambda i: (0, 0)),
        ],
        out_specs=[row_tile, row_tile],
        out_shape=[jax.ShapeDtypeStruct((l, D_MODEL), F32), jax.ShapeDtypeStruct((l, D_MODEL), BF16)],
        compiler_params=_params(("arbitrary",)),
        name="outproj",
    )(merged, w_out, x, lnw)


def _ffn_up_kernel(h_ref, wg_ref, wu_ref, o_ref, wgb_ref, wub_ref):
    _cast_at_first_row_tile(wg_ref, wgb_ref)
    _cast_at_first_row_tile(wu_ref, wub_ref)
    h = h_ref[...]
    gate = jnp.dot(h, wgb_ref[...], preferred_element_type=F32)
    up = jnp.dot(h, wub_ref[...], preferred_element_type=F32)
    o_ref[...] = (_silu(gate) * up).astype(o_ref.dtype)


def _ffn_up(h, w_gate_up, layer, tm, tf):
    l = h.shape[0]
    assert D_FF % tf == 0 and l % tm == 0
    nf = D_FF // tf
    return pl.pallas_call(
        _ffn_up_kernel,
        grid=(nf, l // tm),
        in_specs=[
            pl.BlockSpec((tm, D_MODEL), lambda j, i: (i, 0)),
            pl.BlockSpec((None, D_MODEL, tf), lambda j, i: (layer, 0, j)),
            pl.BlockSpec((None, D_MODEL, tf), lambda j, i: (layer, 0, nf + j)),
        ],
        out_specs=pl.BlockSpec((tm, tf), lambda j, i: (i, j)),
        out_shape=jax.ShapeDtypeStruct((l, D_FF), BF16),
        scratch_shapes=[pltpu.VMEM((D_MODEL, tf), BF16), pltpu.VMEM((D_MODEL, tf), BF16)],
        compiler_params=_params(("arbitrary", "arbitrary")),
        name="ffn_up",
    )(h, w_gate_up, w_gate_up)


def _ffn_down_kernel(a_ref, w_ref, x_ref, o_ref, wb_ref):
    _cast_at_first_row_tile(w_ref, wb_ref)
    o_ref[...] = x_ref[...] + jnp.dot(a_ref[...], wb_ref[...], preferred_element_type=F32)


def _ffn_down(act, w_down, layer, x, tm, tn):
    l = x.shape[0]
    assert D_MODEL % tn == 0 and l % tm == 0
    return pl.pallas_call(
        _ffn_down_kernel,
        grid=(D_MODEL // tn, l // tm),
        in_specs=[
            pl.BlockSpec((tm, D_FF), lambda j, i: (i, 0)),
            pl.BlockSpec((None, D_FF, tn), lambda j, i: (layer, 0, j)),
            pl.BlockSpec((tm, tn), lambda j, i: (i, j)),
        ],
        out_specs=pl.BlockSpec((tm, tn), lambda j, i: (i, j)),
        out_shape=jax.ShapeDtypeStruct((l, D_MODEL), F32),
        scratch_shapes=[pltpu.VMEM((D_FF, tn), BF16)],
        compiler_params=_params(("arbitrary", "arbitrary")),
        name="ffn_down",
    )(act, w_down, x)


def _pad_heads(v):
    return jnp.pad(v, (0, DT_PAD - SSM_HEADS)).reshape(1, DT_PAD)


def _prep_w_in_kernel(a_ref, nxt_ref, o_ref, dt_ref, *, n_main):
    j = pl.program_id(0)

    @pl.when(j < n_main)
    def _():
        o_ref[...] = a_ref[...].T.astype(o_ref.dtype)

    @pl.when(j >= n_main)
    def _():
        rows = jnp.concatenate([a_ref[SSM_HEADS:, :], nxt_ref[...]], axis=0)
        o_ref[...] = rows.T.astype(o_ref.dtype)

    @pl.when(j == n_main)
    def _():
        pad = jnp.zeros((DT_PAD - SSM_HEADS, D_MODEL), F32)
        dt_ref[...] = jnp.concatenate([a_ref[0:SSM_HEADS, :], pad], axis=0).T.astype(dt_ref.dtype)


def _prep_w_in(w_in, layer, tn):
    assert W_MAIN_COLS % tn == 0 and GATE_COLS % tn == 0 and tn % SSM_HEADS == 0 and SSM_HEADS % 8 == 0
    n_main, n_gate = W_MAIN_COLS // tn, GATE_COLS // tn
    w_t = jnp.swapaxes(w_in, 1, 2)
    return pl.pallas_call(
        functools.partial(_prep_w_in_kernel, n_main=n_main),
        grid=(n_main + n_gate,),
        in_specs=[
            pl.BlockSpec((None, tn, D_MODEL), lambda j: (layer, j, 0)),
            pl.BlockSpec((None, SSM_HEADS, D_MODEL), lambda j: (layer, (j + 1) * (tn // SSM_HEADS), 0)),
        ],
        out_specs=[
            pl.BlockSpec((D_MODEL, tn), lambda j: (0, jnp.where(j < n_main, j + n_gate, j - n_main))),
            pl.BlockSpec((D_MODEL, DT_PAD), lambda j: (0, 0)),
        ],
        out_shape=[
            jax.ShapeDtypeStruct((D_MODEL, PROJ_COLS), BF16),
            jax.ShapeDtypeStruct((D_MODEL, DT_PAD), BF16),
        ],
        compiler_params=_params(("arbitrary",)),
        name="prep_w_in",
    )(w_t, w_t)


def kernel(x, ln1_w, w_in, attn_sink, conv_w, conv_b, dt_bias, a_log, d_skip, ssm_norm_w, pool_w, pool_scale,
           w_attn_br, w_pool_br, w_ssm_br, w_out, ln2_w, w_gate_up, w_down, final_w):
    b, l, _ = x.shape
    depth = w_in.shape[0]
    assert l % CHUNK == 0 and CHUNK == ATT_BLOCK
    tm = min(512, l)
    tm_big = min(1024, l)
    row = lambda v: v.reshape(1, -1)
    outs = []
    for bi in range(b):
        xb = x[bi]
        for i in range(depth):
            w_proj, w_dt = _prep_w_in(w_in, i, 768)
            proj, dt_raw = _inproj(xb, row(ln1_w[i]), w_proj, w_dt, tm_big, 1536)
            att = _attention(proj, attn_sink[i])
            pool = _pool(proj, pool_w[i].astype(BF16), row(pool_scale[i]), tm)
            ssm = _ssd(proj, dt_raw,
                       conv_w[i][:, :D_INNER], row(conv_b[i][:D_INNER]),
                       conv_w[i][:, D_INNER:], row(conv_b[i][D_INNER:]),
                       _pad_heads(dt_bias[i]), _pad_heads(a_log[i]),
                       row(jnp.repeat(d_skip[i], SSM_HEAD_DIM)), row(ssm_norm_w[i]))
            merged = _merge(att, pool, ssm, proj, w_attn_br[i].astype(BF16), w_pool_br[i].astype(BF16),
                            w_ssm_br[i].astype(BF16), tm_big, 1024)
            xb, h2 = _outproj(merged, w_out[i].astype(BF16), xb, row(ln2_w[i]), tm)
            act = _ffn_up(h2, w_gate_up, i, tm_big, 512)
            xb = _ffn_down(act, w_down, i, xb, tm, 512)
        outs.append(_final_norm(xb, row(final_w), tm))
    return jnp.stack(outs, axis=0)
```
